```python
import math
import numpy as np
import jax
import jax.numpy as jnp
from jax import lax

D_MODEL = 1024
BATCH = 8
SEQ = 2048
DEPTH = 4
DEC_BATCH = 128
DEC_SEQ = 1
PAST_LEN = 16384
PAGE_SIZE = 128

N_MIXERS = 2
N_SSD_LAYERS = (DEPTH + N_MIXERS - 1) // N_MIXERS
N_CM_LAYERS = DEPTH // N_MIXERS

SSD_EXPAND = 2
D_INNER = SSD_EXPAND * D_MODEL
SSD_HEAD_DIM = 64
SSD_HEADS = D_INNER // SSD_HEAD_DIM
SSD_GROUPS = 8
SSD_HPG = SSD_HEADS // SSD_GROUPS
SSD_STATE = 128
SSD_CONV_K = 4
SSD_CHUNK = 128
SSD_CONV_DIM = D_INNER + 2 * SSD_GROUPS * SSD_STATE
SSD_IN_DIM = D_INNER + SSD_CONV_DIM + SSD_HEADS

CM_CHUNK = 128
CM_WIDTH = 2 * D_MODEL
CM_GROUPS = 8
CM_GROUP_DIM = CM_WIDTH // CM_GROUPS

D_FF = 4 * D_MODEL

RMS_EPS = 1e-5

kernel_name = 'hybrid_ssd_chunkmlp_decoder_step'


def rmsnorm(x, g):
    xf = x.astype(jnp.float32)
    r = lax.rsqrt(jnp.mean(xf * xf, axis=-1, keepdims=True) + RMS_EPS)
    return (xf * r * g.astype(jnp.float32)).astype(x.dtype)


def ssd_scan(x, dt, A, B, C, h0):
    b, l = x.shape[:2]
    q = min(SSD_CHUNK, l)
    lp = -(-l // q) * q
    pad = lp - l
    if pad:
        pw = ((0, 0), (0, pad))
        x = jnp.pad(x, pw + ((0, 0), (0, 0)))
        dt = jnp.pad(dt, pw + ((0, 0),))
        B = jnp.pad(B, pw + ((0, 0), (0, 0)))
        C = jnp.pad(C, pw + ((0, 0), (0, 0)))
    nc = lp // q
    xc = x.reshape(b, nc, q, SSD_GROUPS, SSD_HPG, SSD_HEAD_DIM)
    dtc = dt.astype(jnp.float32).reshape(b, nc, q, SSD_GROUPS, SSD_HPG)
    Bc = B.reshape(b, nc, q, SSD_GROUPS, SSD_STATE)
    Cc = C.reshape(b, nc, q, SSD_GROUPS, SSD_STATE)
    a = dtc * A.astype(jnp.float32).reshape(SSD_GROUPS, SSD_HPG)
    a_cum = jnp.cumsum(a, axis=2)
    xdt = xc * dtc[..., None]
    ac = jnp.moveaxis(a_cum, 2, -1)
    seg = ac[..., :, None] - ac[..., None, :]
    causal = np.tril(np.ones((q, q), dtype=bool))
    decay = jnp.exp(jnp.where(causal, seg, -jnp.inf))
    cb = jnp.einsum('bcqgn,bcsgn->bcgqs', Cc, Bc)
    y_diag = jnp.einsum('bcgrqs,bcsgrp->bcqgrp', cb[:, :, :, None] * decay, xdt)
    decay_to_end = jnp.exp(a_cum[:, :, -1:] - a_cum)
    chunk_states = jnp.einsum('bcsgn,bcsgr,bcsgrp->bcgrpn', Bc, decay_to_end, xdt)
    chunk_decay = jnp.exp(a_cum[:, :, -1])

    def step(h, inp):
        s, d = inp
        return h * d[..., None, None] + s, h

    h_final, h_prev = lax.scan(step, h0.astype(jnp.float32),
                               (jnp.moveaxis(chunk_states, 1, 0), jnp.moveaxis(chunk_decay, 1, 0)))
    h_prev = jnp.moveaxis(h_prev, 0, 1)
    y_off = jnp.einsum('bcqgn,bcgrpn,bcqgr->bcqgrp', Cc, h_prev, jnp.exp(a_cum))
    y = (y_diag + y_off).reshape(b, lp, SSD_HEADS, SSD_HEAD_DIM)[:, :l]
    return y, h_final


def mamba2_mixer(h, conv_buf, ssm_state, w_in, conv_w, conv_b, dt_bias, a_log, d_skip, norm_g, w_out):
    b, l, _ = h.shape
    proj = h @ w_in
    z = proj[..., :D_INNER]
    xbc = proj[..., D_INNER:D_INNER + SSD_CONV_DIM]
    dt_raw = proj[..., D_INNER + SSD_CONV_DIM:]
    xbc_ext = jnp.concatenate([conv_buf.astype(xbc.dtype), xbc], axis=1)
    conv = conv_b + xbc_ext[:, 0:l] * conv_w[0]
    for k in range(1, SSD_CONV_K):
        conv = conv + xbc_ext[:, k:k + l] * conv_w[k]
    new_conv = xbc_ext[:, -(SSD_CONV_K - 1):]
    xbc_act = jax.nn.silu(conv)
    xs = xbc_act[..., :D_INNER].reshape(b, l, SSD_HEADS, SSD_HEAD_DIM)
    gn = SSD_GROUPS * SSD_STATE
    Bm = xbc_act[..., D_INNER:D_INNER + gn].reshape(b, l, SSD_GROUPS, SSD_STATE)
    Cm = xbc_act[..., D_INNER + gn:].reshape(b, l, SSD_GROUPS, SSD_STATE)
    dt = jax.nn.softplus(dt_raw.astype(jnp.float32) + dt_bias.astype(jnp.float32))
    A = -jnp.exp(a_log.astype(jnp.float32))
    h0 = ssm_state.reshape(b, SSD_GROUPS, SSD_HPG, SSD_HEAD_DIM, SSD_STATE)
    y, h_final = ssd_scan(xs, dt, A, Bm, Cm, h0)
    y = y + xs * d_skip[:, None]
    y = y.reshape(b, l, D_INNER) * jax.nn.silu(z)
    yg = y.astype(jnp.float32).reshape(b, l, SSD_GROUPS, D_INNER // SSD_GROUPS)
    yg = yg * lax.rsqrt(jnp.mean(yg * yg, axis=-1, keepdims=True) + RMS_EPS)
    y = (yg.reshape(b, l, D_INNER) * norm_g).astype(h.dtype)
    new_ssm = h_final.reshape(b, SSD_HEADS, SSD_HEAD_DIM, SSD_STATE)
    return y @ w_out, new_conv, new_ssm


def chunk_mlp_mixer(h, chunk_len, pos, w_in, b_in, v_g, w_s, b_s, w_out):
    b, l, _ = h.shape
    n = l // chunk_len
    gz = jax.nn.gelu(h @ w_in + b_in, approximate=False)
    u = gz[..., :CM_WIDTH]
    v = rmsnorm(gz[..., CM_WIDTH:], v_g)
    mask = np.tril(np.ones((chunk_len, chunk_len), dtype=bool))
    w = jnp.where(mask, w_s[:, pos][:, :, pos], 0)
    vc = v.reshape(b, n, chunk_len, CM_GROUPS, CM_GROUP_DIM)
    s = jnp.einsum('gts,bnsgd->bntgd', w, vc) + b_s[:, pos].T[:, :, None]
    out = u * s.reshape(b, l, CM_WIDTH).astype(u.dtype)
    return out @ w_out, v


def squared_relu_mlp(h, w_up, w_down):
    a = jax.nn.relu(h @ w_up)
    return (a * a) @ w_down


def trunk(x, conv0, ssm0, chunk_len, pos, p):
    new_conv, new_ssm, new_v = [], [], []
    for i in range(DEPTH):
        j = i // N_MIXERS
        h = rmsnorm(x, p['norm_mix_g'][i])
        if i % N_MIXERS == 0:
            out, c, s = mamba2_mixer(h, conv0[j], ssm0[j], p['ssd_w_in'][j], p['ssd_conv_w'][j],
                                     p['ssd_conv_b'][j], p['ssd_dt_bias'][j], p['ssd_a_log'][j],
                                     p['ssd_d'][j], p['ssd_norm_g'][j], p['ssd_w_out'][j])
            new_conv.append(c)
            new_ssm.append(s)
        else:
            out, v = chunk_mlp_mixer(h, chunk_len, pos, p['cm_w_in'][j], p['cm_b_in'][j],
                                     p['cm_v_norm_g'][j], p['cm_w_s'][j], p['cm_b_s'][j],
                                     p['cm_w_out'][j])
            new_v.append(v)
        x = x + out.astype(x.dtype)
        h2 = rmsnorm(x, p['norm_mlp_g'][i])
        x = x + squared_relu_mlp(h2, p['mlp_w_up'][i], p['mlp_w_down'][i]).astype(x.dtype)
    y = rmsnorm(x, p['norm_final_g'])
    return y, jnp.stack(new_conv), jnp.stack(new_ssm), jnp.stack(new_v)


def setup_inputs(seed: int = 0) -> dict:
    key = jax.random.key(seed)
    ks = jax.random.split(key, 24)
    f32 = jnp.float32
    nrm = lambda k, shape, scale: jax.random.normal(k, shape, f32) * scale
    x_prompt = nrm(ks[0], (BATCH, SEQ, D_MODEL), 1.0)
    x_sample = nrm(ks[1], (DEC_BATCH, DEC_SEQ, D_MODEL), 1.0)
    state_ssm = nrm(ks[2], (N_SSD_LAYERS, DEC_BATCH, SSD_HEADS, SSD_HEAD_DIM, SSD_STATE), 0.5)
    state_conv = nrm(ks[3], (N_SSD_LAYERS, DEC_BATCH, SSD_CONV_K - 1, SSD_CONV_DIM), 1.0)
    norm_mix_g = 1.0 + nrm(ks[4], (DEPTH, D_MODEL), 0.02)
    norm_mlp_g = 1.0 + nrm(ks[5], (DEPTH, D_MODEL), 0.02)
    norm_final_g = 1.0 + nrm(ks[6], (D_MODEL,), 0.02)
    ssd_w_in = nrm(ks[7], (N_SSD_LAYERS, D_MODEL, SSD_IN_DIM), D_MODEL ** -0.5)
    ssd_conv_w = nrm(ks[8], (N_SSD_LAYERS, SSD_CONV_K, SSD_CONV_DIM), SSD_CONV_K ** -0.5)
    ssd_conv_b = nrm(ks[9], (N_SSD_LAYERS, SSD_CONV_DIM), 0.01)
    u = jax.random.uniform(ks[10], (N_SSD_LAYERS, SSD_HEADS), f32)
    dt0 = jnp.exp(u * (math.log(0.1) - math.log(0.001)) + math.log(0.001))
    ssd_dt_bias = dt0 + jnp.log(-jnp.expm1(-dt0))
    ssd_a_log = jnp.log(jax.random.uniform(ks[11], (N_SSD_LAYERS, SSD_HEADS), f32, 1.0, 16.0))
    ssd_d = 1.0 + nrm(ks[12], (N_SSD_LAYERS, SSD_HEADS), 0.1)
    ssd_norm_g = 1.0 + nrm(ks[13], (N_SSD_LAYERS, D_INNER), 0.02)
    ssd_w_out = nrm(ks[14], (N_SSD_LAYERS, D_INNER, D_MODEL), D_INNER ** -0.5)
    cm_w_in = nrm(ks[15], (N_CM_LAYERS, D_MODEL, 2 * CM_WIDTH), D_MODEL ** -0.5)
    cm_b_in = nrm(ks[16], (N_CM_LAYERS, 2 * CM_WIDTH), 0.01)
    cm_v_norm_g = 1.0 + nrm(ks[17], (N_CM_LAYERS, CM_WIDTH), 0.02)
    row_scale = (jnp.arange(CM_CHUNK, dtype=f32) + 1.0) ** -0.5
    cm_w_s = nrm(ks[18], (N_CM_LAYERS, CM_GROUPS, CM_CHUNK, CM_CHUNK), 1.0) * row_scale[:, None]
    cm_b_s = 1.0 + nrm(ks[19], (N_CM_LAYERS, CM_GROUPS, CM_CHUNK), 0.1)
    cm_w_out = nrm(ks[20], (N_CM_LAYERS, CM_WIDTH, D_MODEL), CM_WIDTH ** -0.5)
    mlp_w_up = nrm(ks[21], (DEPTH, D_MODEL, D_FF), D_MODEL ** -0.5)
    mlp_w_down = nrm(ks[22], (DEPTH, D_FF, D_MODEL), D_FF ** -0.5)
    return {'x_prompt': x_prompt, 'x_sample': x_sample, 'state_ssm': state_ssm,
            'state_conv': state_conv, 'norm_mix_g': norm_mix_g, 'norm_mlp_g': norm_mlp_g,
            'norm_final_g': norm_final_g, 'ssd_w_in': ssd_w_in, 'ssd_conv_w': ssd_conv_w,
            'ssd_conv_b': ssd_conv_b, 'ssd_dt_bias': ssd_dt_bias, 'ssd_a_log': ssd_a_log,
            'ssd_d': ssd_d, 'ssd_norm_g': ssd_norm_g, 'ssd_w_out': ssd_w_out,
            'cm_w_in': cm_w_in, 'cm_b_in': cm_b_in, 'cm_v_norm_g': cm_v_norm_g,
            'cm_w_s': cm_w_s, 'cm_b_s': cm_b_s, 'cm_w_out': cm_w_out,
            'mlp_w_up': mlp_w_up, 'mlp_w_down': mlp_w_down}


def reference(x_prompt, x_sample, state_ssm, state_conv, norm_mix_g, norm_mlp_g, norm_final_g,
              ssd_w_in, ssd_conv_w, ssd_conv_b, ssd_dt_bias, ssd_a_log, ssd_d, ssd_norm_g,
              ssd_w_out, cm_w_in, cm_b_in, cm_v_norm_g, cm_w_s, cm_b_s, cm_w_out,
              mlp_w_up, mlp_w_down):
    p = {'norm_mix_g': norm_mix_g, 'norm_mlp_g': norm_mlp_g, 'norm_final_g': norm_final_g,
         'ssd_w_in': ssd_w_in, 'ssd_conv_w': ssd_conv_w, 'ssd_conv_b': ssd_conv_b,
         'ssd_dt_bias': ssd_dt_bias, 'ssd_a_log': ssd_a_log, 'ssd_d': ssd_d,
         'ssd_norm_g': ssd_norm_g, 'ssd_w_out': ssd_w_out, 'cm_w_in': cm_w_in,
         'cm_b_in': cm_b_in, 'cm_v_norm_g': cm_v_norm_g, 'cm_w_s': cm_w_s, 'cm_b_s': cm_b_s,
         'cm_w_out': cm_w_out, 'mlp_w_up': mlp_w_up, 'mlp_w_down': mlp_w_down}
    b = x_prompt.shape[0]
    conv0 = jnp.zeros((N_SSD_LAYERS, b, SSD_CONV_K - 1, SSD_CONV_DIM), x_prompt.dtype)
    ssm0 = jnp.zeros((N_SSD_LAYERS, b, SSD_HEADS, SSD_HEAD_DIM, SSD_STATE), jnp.float32)
    y_prompt, conv_p, ssm_p, _ = trunk(x_prompt, conv0, ssm0, CM_CHUNK, np.arange(CM_CHUNK), p)
    pos_s = (PAST_LEN + np.arange(DEC_SEQ)) % CM_CHUNK
    y_sample, conv_s, ssm_s, v_s = trunk(x_sample, state_conv, state_ssm, DEC_SEQ, pos_s, p)
    return (y_prompt, y_sample, ssm_p, conv_p, ssm_s, conv_s, v_s)
```

```python
import functools

import numpy as np
import jax
import jax.numpy as jnp
from jax import lax
from jax.experimental import pallas as pl
from jax.experimental.pallas import tpu as pltpu

F32 = jnp.float32
BF16 = jnp.bfloat16

D_MODEL = 1024
DEPTH = 4
D_INNER = 2048
HEAD_DIM = 64
HEADS = 32
GROUPS = 8
HEADS_PER_GROUP = 4
STATE = 128
CONV_K = 4
CHUNK = 128
CONV_DIM = D_INNER + 2 * GROUPS * STATE
GROUP_WIDTH = D_INNER // GROUPS
CM_CHUNK = 128
CM_WIDTH = 2 * D_MODEL
CM_GROUPS = 8
CM_GROUP_DIM = CM_WIDTH // CM_GROUPS
D_FF = 4 * D_MODEL
PAST_LEN = 16384
RMS_EPS = 1e-5

LANES = 128
VMEM_LIMIT = 56 * 1024 * 1024
SSD_ROWS = 256
CM_ROWS = 256
MLP_ROWS = 512
FF_BLOCK = 1024
COL_BLOCK = 512
SQRT_HALF = np.sqrt(0.5).astype(np.float32)


def _rmsnorm(x, g):
    ms = jnp.mean(x * x, axis=-1, keepdims=True)
    return x * lax.rsqrt(ms + RMS_EPS) * g


def _gelu(x):
    return 0.5 * x * (1.0 + lax.erf(x * SQRT_HALF))


def _dot(a, b):
    return jnp.dot(a, b, preferred_element_type=F32)


def _resident(shape):
    nd = len(shape)
    return pl.BlockSpec(shape, lambda *_: (0,) * nd, pipeline_mode=pl.Buffered(1))


def _params(n_axes):
    return pltpu.CompilerParams(
        dimension_semantics=("arbitrary",) * n_axes, vmem_limit_bytes=VMEM_LIMIT)


def _mlp_kernel(x_ref, g_ref, wup_ref, wdown_ref, gf_ref, o_ref, *, final):
    x = x_ref[...]
    hn = _rmsnorm(x, g_ref[...]).astype(BF16)
    acc = x
    for c in range(D_FF // FF_BLOCK):
        cols = slice(c * FF_BLOCK, (c + 1) * FF_BLOCK)
        a = jnp.maximum(_dot(hn, wup_ref[:, cols]), 0.0)
        acc = acc + _dot((a * a).astype(BF16), wdown_ref[cols, :])
    if final:
        acc = _rmsnorm(acc, gf_ref[...])
    o_ref[...] = acc


def _mlp(x, g, w_up, w_down, g_final, final):
    n = x.shape[0]
    rows = min(MLP_ROWS, n)
    return pl.pallas_call(
        functools.partial(_mlp_kernel, final=final),
        out_shape=jax.ShapeDtypeStruct((n, D_MODEL), F32),
        grid=(n // rows,),
        in_specs=[pl.BlockSpec((rows, D_MODEL), lambda i: (i, 0)),
                  _resident((1, D_MODEL)),
                  _resident((D_MODEL, D_FF)),
                  _resident((D_FF, D_MODEL)),
                  _resident((1, D_MODEL))],
        out_specs=pl.BlockSpec((rows, D_MODEL), lambda i: (i, 0)),
        compiler_params=_params(1),
        name="mlp",
    )(x, g, w_up, w_down, g_final)


def _cm_prompt_kernel(x_ref, g_ref, win_ref, bin_ref, vg_ref, ws_ref, bs_ref, wout_ref,
                      o_ref, u_sc, v_sc, y_sc):
    rows = x_ref.shape[0]
    x = x_ref[...]
    hn = _rmsnorm(x, g_ref[...]).astype(BF16)
    for c in range(CM_WIDTH // COL_BLOCK):
        cols = slice(c * COL_BLOCK, (c + 1) * COL_BLOCK)
        u_sc[:, cols] = _gelu(_dot(hn, win_ref[:, cols]) + bin_ref[:, cols])
    ss = jnp.zeros((rows, 1), F32)
    for c in range(CM_WIDTH // COL_BLOCK):
        cols = slice(c * COL_BLOCK, (c + 1) * COL_BLOCK)
        wcols = slice(CM_WIDTH + c * COL_BLOCK, CM_WIDTH + (c + 1) * COL_BLOCK)
        gv = _gelu(_dot(hn, win_ref[:, wcols]) + bin_ref[:, wcols])
        ss = ss + jnp.sum(gv * gv, axis=-1, keepdims=True)
        v_sc[:, cols] = gv
    r = lax.rsqrt(ss * (1.0 / CM_WIDTH) + RMS_EPS)
    t_idx = lax.broadcasted_iota(jnp.int32, (CM_CHUNK, CM_CHUNK), 0)
    s_idx = lax.broadcasted_iota(jnp.int32, (CM_CHUNK, CM_CHUNK), 1)
    causal = t_idx >= s_idx
    for g in range(CM_GROUPS):
        cols = slice(g * CM_GROUP_DIM, (g + 1) * CM_GROUP_DIM)
        w = jnp.where(causal, ws_ref[g], 0.0).astype(BF16)
        for c in range(rows // CM_CHUNK):
            rs = slice(c * CM_CHUNK, (c + 1) * CM_CHUNK)
            v = (v_sc[rs, cols] * r[rs] * vg_ref[:, cols]).astype(BF16)
            s = _dot(w, v) + bs_ref[:, cols]
            y_sc[rs, cols] = (u_sc[rs, cols] * s).astype(BF16)
    o_ref[...] = x + _dot(y_sc[...], wout_ref[...])


def _cm_prompt(x, g, w_in, b_in, v_g, w_s, b_s_wide, w_out):
    n = x.shape[0]
    rows = min(CM_ROWS, n)
    return pl.pallas_call(
        _cm_prompt_kernel,
        out_shape=jax.ShapeDtypeStruct((n, D_MODEL), F32),
        grid=(n // rows,),
        in_specs=[pl.BlockSpec((rows, D_MODEL), lambda i: (i, 0)),
                  _resident((1, D_MODEL)),
                  _resident((D_MODEL, 2 * CM_WIDTH)),
                  _resident((1, 2 * CM_WIDTH)),
                  _resident((1, CM_WIDTH)),
                  _resident((CM_GROUPS, CM_CHUNK, CM_CHUNK)),
                  _resident((CM_CHUNK, CM_WIDTH)),
                  _resident((CM_WIDTH, D_MODEL))],
        out_specs=pl.BlockSpec((rows, D_MODEL), lambda i: (i, 0)),
        scratch_shapes=[pltpu.VMEM((rows, CM_WIDTH), F32),
                        pltpu.VMEM((rows, CM_WIDTH), F32),
                        pltpu.VMEM((rows, CM_WIDTH), BF16)],
        compiler_params=_params(1),
        name="cm_prompt",
    )(x, g, w_in, b_in, v_g, w_s, b_s_wide, w_out)


def _cm_sample_kernel(x_ref, g_ref, win_ref, bin_ref, vg_ref, wdiag_ref, bdiag_ref, wout_ref,
                      o_ref, v_ref):
    x = x_ref[...]
    hn = _rmsnorm(x, g_ref[...]).astype(BF16)
    u = _gelu(_dot(hn, win_ref[:, :CM_WIDTH]) + bin_ref[:, :CM_WIDTH])
    gv = _gelu(_dot(hn, win_ref[:, CM_WIDTH:]) + bin_ref[:, CM_WIDTH:])
    v = _rmsnorm(gv, vg_ref[...])
    v_ref[...] = v
    s = v * wdiag_ref[...] + bdiag_ref[...]
    o_ref[...] = x + _dot((u * s).astype(BF16), wout_ref[...])


def _cm_sample(x, g, w_in, b_in, v_g, w_diag, b_diag, w_out):
    n = x.shape[0]
    return pl.pallas_call(
        _cm_sample_kernel,
        out_shape=(jax.ShapeDtypeStruct((n, D_MODEL), F32),
                   jax.ShapeDtypeStruct((n, CM_WIDTH), F32)),
        grid=(1,),
        in_specs=[_resident((n, D_MODEL)),
                  _resident((1, D_MODEL)),
                  _resident((D_MODEL, 2 * CM_WIDTH)),
                  _resident((1, 2 * CM_WIDTH)),
                  _resident((1, CM_WIDTH)),
                  _resident((1, CM_WIDTH)),
                  _resident((1, CM_WIDTH)),
                  _resident((CM_WIDTH, D_MODEL))],
        out_specs=(pl.BlockSpec((n, D_MODEL), lambda i: (0, 0)),
                   pl.BlockSpec((n, CM_WIDTH), lambda i: (0, 0))),
        compiler_params=_params(1),
        name="cm_sample",
    )(x, g, w_in, b_in, v_g, w_diag, b_diag, w_out)


def _ssd_prompt_kernel(x_ref, g_ref, wz_ref, wxbc_ref, wdt_ref, convw_ref, convb_ref, dtb_ref,
                       alog_ref, dskip_ref, ng_ref, wout_ref,
                       o_ref, ssm_ref, convo_ref,
                       ext_sc, z_sc, act_sc, yn_sc, ht_sc):
    rows = x_ref.shape[0]
    step = pl.program_id(1)
    pad = 8

    @pl.when(step == 0)
    def _():
        ext_sc[0:pad, :] = jnp.zeros((pad, CONV_DIM), F32)
        ht_sc[...] = jnp.zeros(ht_sc.shape, F32)

    x = x_ref[...]
    hn = _rmsnorm(x, g_ref[...]).astype(BF16)
    z_sc[...] = _dot(hn, wz_ref[...])
    ext_sc[pad:pad + rows, :] = _dot(hn, wxbc_ref[...])
    dt_raw = _dot(hn, wdt_ref[...])

    for c in range(CONV_DIM // COL_BLOCK):
        cols = slice(c * COL_BLOCK, (c + 1) * COL_BLOCK)
        conv = convb_ref[:, cols] + ext_sc[pad - 3:pad - 3 + rows, cols] * convw_ref[0:1, cols]
        for k in range(1, CONV_K):
            conv = conv + ext_sc[pad - 3 + k:pad - 3 + k + rows, cols] * convw_ref[k:k + 1, cols]
        act_sc[:, cols] = conv * jax.nn.sigmoid(conv)
    tail = ext_sc[pad + rows - 3:pad + rows, :]
    convo_ref[0] = tail
    ext_sc[pad - 3:pad, :] = tail

    a_row = -jnp.exp(alog_ref[...])
    q_idx = lax.broadcasted_iota(jnp.int32, (CHUNK, CHUNK), 0)
    s_idx = lax.broadcasted_iota(jnp.int32, (CHUNK, CHUNK), 1)
    causal = q_idx >= s_idx
    tril = causal.astype(F32)
    low = s_idx < HEAD_DIM
    b_off = D_INNER
    c_off = D_INNER + GROUPS * STATE

    for c in range(rows // CHUNK):
        rs = slice(c * CHUNK, (c + 1) * CHUNK)
        dt = jax.nn.softplus(dt_raw[rs] + dtb_ref[...])
        a = dt * a_row
        acum = jnp.dot(tril, a, precision=lax.Precision.HIGHEST, preferred_element_type=F32)
        acum_t = acum.T
        dt_t = dt.T
        w_t = jnp.exp(acum_t[:, CHUNK - 1:CHUNK] - acum_t) * dt_t
        cdec = jnp.exp(acum[CHUNK - 1:CHUNK, :])
        for g in range(GROUPS):
            bg = act_sc[rs, b_off + g * STATE:b_off + (g + 1) * STATE]
            cg = act_sc[rs, c_off + g * STATE:c_off + (g + 1) * STATE].astype(BF16)
            bt = bg.T
            cb = _dot(cg, bt.astype(BF16))
            ht = ht_sc[g]
            y_off = _dot(cg, ht.astype(BF16))
            ys = []
            for pr in range(HEADS_PER_GROUP // 2):
                h0 = g * HEADS_PER_GROUP + 2 * pr
                lanes = slice(h0 * HEAD_DIM, (h0 + 2) * HEAD_DIM)
                half = slice(pr * LANES, (pr + 1) * LANES)
                xs = act_sc[rs, lanes]
                rhs = jnp.concatenate([jnp.where(low, xs, 0.0).astype(BF16),
                                       jnp.where(low, 0.0, xs).astype(BF16)], axis=0)
                ms, bs, cols_b = [], [], []
                for hd in (h0, h0 + 1):
                    col_b = jnp.broadcast_to(acum[:, hd:hd + 1], (CHUNK, CHUNK))
                    seg = col_b - acum_t[hd:hd + 1, :]
                    decay = jnp.exp(jnp.where(causal, seg, -jnp.inf))
                    ms.append((cb * decay * dt_t[hd:hd + 1, :]).astype(BF16))
                    bs.append((bt * w_t[hd:hd + 1, :]).astype(BF16))
                    cols_b.append(col_b)
                lhs = jnp.concatenate([jnp.concatenate(ms, axis=1),
                                       jnp.concatenate(bs, axis=1)], axis=0)
                res = _dot(lhs, rhs)
                e = jnp.exp(jnp.where(low, cols_b[0], cols_b[1]))
                y = res[0:CHUNK] + e * y_off[:, half] + xs * dskip_ref[:, lanes]
                cd = jnp.where(low[0:1], cdec[:, h0:h0 + 1], cdec[:, h0 + 1:h0 + 2])
                ht_sc[g, :, half] = ht[:, half] * cd + res[CHUNK:2 * CHUNK]
                ys.append(y)
            gcols = slice(g * GROUP_WIDTH, (g + 1) * GROUP_WIDTH)
            yg = jnp.concatenate(ys, axis=1)
            zg = z_sc[rs, gcols]
            yg = yg * (zg * jax.nn.sigmoid(zg))
            yg = yg * lax.rsqrt(jnp.mean(yg * yg, axis=-1, keepdims=True) + RMS_EPS)
            yn_sc[rs, gcols] = (yg * ng_ref[:, gcols]).astype(BF16)

    o_ref[...] = x + _dot(yn_sc[...], wout_ref[...])

    @pl.when(step == pl.num_programs(1) - 1)
    def _():
        for g in range(GROUPS):
            ssm_ref[0, g * GROUP_WIDTH:(g + 1) * GROUP_WIDTH, :] = ht_sc[g].T


def _ssd_prompt(x, batch, g, w_z, w_xbc, w_dt, conv_w, conv_b, dt_bias, a_log, d_wide, norm_g, w_out):
    n = x.shape[0]
    seq = n // batch
    rows = min(SSD_ROWS, seq)
    steps = seq // rows
    return pl.pallas_call(
        _ssd_prompt_kernel,
        out_shape=(jax.ShapeDtypeStruct((n, D_MODEL), F32),
                   jax.ShapeDtypeStruct((batch, D_INNER, STATE), F32),
                   jax.ShapeDtypeStruct((batch, CONV_K - 1, CONV_DIM), F32)),
        grid=(batch, steps),
        in_specs=[pl.BlockSpec((rows, D_MODEL), lambda b, s: (b * steps + s, 0)),
                  _resident((1, D_MODEL)),
                  _resident((D_MODEL, D_INNER)),
                  _resident((D_MODEL, CONV_DIM)),
                  _resident((D_MODEL, LANES)),
                  _resident((CONV_K, CONV_DIM)),
                  _resident((1, CONV_DIM)),
                  _resident((1, LANES)),
                  _resident((1, LANES)),
                  _resident((1, D_INNER)),
                  _resident((1, D_INNER)),
                  _resident((D_INNER, D_MODEL))],
        out_specs=(pl.BlockSpec((rows, D_MODEL), lambda b, s: (b * steps + s, 0)),
                   pl.BlockSpec((1, D_INNER, STATE), lambda b, s: (b, 0, 0)),
                   pl.BlockSpec((1, CONV_K - 1, CONV_DIM), lambda b, s: (b, 0, 0))),
        scratch_shapes=[pltpu.VMEM((rows + 8, CONV_DIM), F32),
                        pltpu.VMEM((rows, D_INNER), F32),
                        pltpu.VMEM((rows, CONV_DIM), F32),
                        pltpu.VMEM((rows, D_INNER), BF16),
                        pltpu.VMEM((GROUPS, STATE, GROUP_WIDTH), F32)],
        compiler_params=_params(2),
        name="ssd_prompt",
    )(x, g, w_z, w_xbc, w_dt, conv_w, conv_b, dt_bias, a_log, d_wide, norm_g, w_out)


def _ssd_sample_in_kernel(x_ref, g_ref, wz_ref, wxbc_ref, wdt_ref, cs_ref, convw_ref, convb_ref,
                          dtb_ref, alog_ref, expand_ref,
                          z_ref, xs_ref, b_ref, c_ref, xdt_ref, dec_ref, convo_ref):
    hn = _rmsnorm(x_ref[...], g_ref[...]).astype(BF16)
    z_ref[...] = _dot(hn, wz_ref[...])
    xbc = _dot(hn, wxbc_ref[...])
    dt_raw = _dot(hn, wdt_ref[...])
    conv = convb_ref[...] + cs_ref[:, 0:CONV_DIM] * convw_ref[0:1, :]
    for k in range(1, CONV_K - 1):
        conv = conv + cs_ref[:, k * CONV_DIM:(k + 1) * CONV_DIM] * convw_ref[k:k + 1, :]
    conv = conv + xbc * convw_ref[CONV_K - 1:CONV_K, :]
    convo_ref[:, 0:(CONV_K - 2) * CONV_DIM] = cs_ref[:, CONV_DIM:(CONV_K - 1) * CONV_DIM]
    convo_ref[:, (CONV_K - 2) * CONV_DIM:] = xbc
    act = conv * jax.nn.sigmoid(conv)
    xs = act[:, :D_INNER]
    xs_ref[...] = xs
    b_ref[...] = act[:, D_INNER:D_INNER + GROUPS * STATE]
    c_ref[...] = act[:, D_INNER + GROUPS * STATE:]
    dt = jax.nn.softplus(dt_raw + dtb_ref[...])
    dec = jnp.exp(dt * (-jnp.exp(alog_ref[...])))

    def widen(v):
        hi = v.astype(BF16)
        lo = (v - hi.astype(F32)).astype(BF16)
        return _dot(hi, expand_ref[...]) + _dot(lo, expand_ref[...])

    xdt_ref[...] = xs * widen(dt)
    dec_ref[...] = widen(dec)


def _ssd_sample_state_kernel(h_ref, xdt_ref, dec_ref, b_ref, c_ref, ho_ref, y_ref):
    g_idx = lax.broadcasted_iota(jnp.int32, (GROUPS, D_INNER), 0)
    l_idx = lax.broadcasted_iota(jnp.int32, (GROUPS, D_INNER), 1)
    own = (l_idx // GROUP_WIDTH) == g_idx
    xdt = xdt_ref[0]
    dec = dec_ref[0]
    u_t = jnp.where(own, xdt, 0.0).astype(BF16)
    contract0 = (((0,), (0,)), ((), ()))
    new = lax.dot_general(u_t, b_ref[0].astype(BF16), contract0, preferred_element_type=F32)
    hi = dec.astype(BF16).astype(F32)
    lo = dec - hi
    r_idx = lax.broadcasted_iota(jnp.int32, (16, D_INNER), 0)
    d_t = jnp.where(r_idx == 0, hi, jnp.where(r_idx == 1, lo, 0.0)).astype(BF16)
    dfull = lax.dot_general(d_t, jnp.ones((16, STATE), BF16), contract0, preferred_element_type=F32)
    h_new = h_ref[0] * dfull + new
    ho_ref[0] = h_new
    y_all = lax.dot_general(c_ref[0].astype(BF16), h_new.astype(BF16),
                            (((1,), (1,)), ((), ())), preferred_element_type=F32)
    y_ref[0] = jnp.sum(jnp.where(own, y_all, 0.0), axis=0, keepdims=True)


def _ssd_sample_out_kernel(x_ref, y_ref, xs_ref, z_ref, dskip_ref, ng_ref, wout_ref, o_ref):
    y = y_ref[...] + xs_ref[...] * dskip_ref[...]
    z = z_ref[...]
    y = y * (z * jax.nn.sigmoid(z))
    parts = []
    for g in range(GROUPS):
        yg = y[:, g * GROUP_WIDTH:(g + 1) * GROUP_WIDTH]
        parts.append(yg * lax.rsqrt(jnp.mean(yg * yg, axis=-1, keepdims=True) + RMS_EPS))
    yn = (jnp.concatenate(parts, axis=1) * ng_ref[...]).astype(BF16)
    o_ref[...] = x_ref[...] + _dot(yn, wout_ref[...])


def _ssd_sample(x, ssm_all, conv_all, layer, g, w_z, w_xbc, w_dt, conv_w, conv_b, dt_bias, a_log,
                expand, d_wide, norm_g, w_out):
    n = x.shape[0]
    n_layers = ssm_all.shape[0]
    gs = GROUPS * STATE
    full = lambda shape: pl.BlockSpec(shape, lambda i: (0,) * len(shape))
    z, xs, bm, cm, xdt, dec, conv_new = pl.pallas_call(
        _ssd_sample_in_kernel,
        out_shape=(jax.ShapeDtypeStruct((n, D_INNER), F32),
                   jax.ShapeDtypeStruct((n, D_INNER), F32),
                   jax.ShapeDtypeStruct((n, gs), F32),
                   jax.ShapeDtypeStruct((n, gs), F32),
                   jax.ShapeDtypeStruct((n, D_INNER), F32),
                   jax.ShapeDtypeStruct((n, D_INNER), F32),
                   jax.ShapeDtypeStruct((n, (CONV_K - 1) * CONV_DIM), F32)),
        grid=(1,),
        in_specs=[_resident((n, D_MODEL)),
                  _resident((1, D_MODEL)),
                  _resident((D_MODEL, D_INNER)),
                  _resident((D_MODEL, CONV_DIM)),
                  _resident((D_MODEL, LANES)),
                  pl.BlockSpec((n, (CONV_K - 1) * CONV_DIM), lambda i: (layer, 0),
                               pipeline_mode=pl.Buffered(1)),
                  _resident((CONV_K, CONV_DIM)),
                  _resident((1, CONV_DIM)),
                  _resident((1, LANES)),
                  _resident((1, LANES)),
                  _resident((LANES, D_INNER))],
        out_specs=(full((n, D_INNER)), full((n, D_INNER)), full((n, gs)), full((n, gs)),
                   full((n, D_INNER)), full((n, D_INNER)), full((n, (CONV_K - 1) * CONV_DIM))),
        compiler_params=_params(1),
        name="ssd_sample_in",
    )(x, g, w_z, w_xbc, w_dt, conv_all.reshape(n_layers * n, (CONV_K - 1) * CONV_DIM), conv_w, conv_b,
      dt_bias, a_log, expand)

    row = lambda width: pl.BlockSpec((1, 1, width), lambda i: (i, 0, 0))
    grp = pl.BlockSpec((1, GROUPS, STATE), lambda i: (i, 0, 0))
    state = pl.BlockSpec((1, D_INNER, STATE), lambda i: (i, 0, 0))
    ssm_new, y = pl.pallas_call(
        _ssd_sample_state_kernel,
        out_shape=(jax.ShapeDtypeStruct((n, D_INNER, STATE), F32),
                   jax.ShapeDtypeStruct((n, 1, D_INNER), F32)),
        grid=(n,),
        in_specs=[pl.BlockSpec((1, D_INNER, STATE), lambda i: (layer * n + i, 0, 0)),
                  row(D_INNER), row(D_INNER), grp, grp],
        out_specs=(state, row(D_INNER)),
        compiler_params=_params(1),
        name="ssd_sample_state",
    )(ssm_all.reshape(n_layers * n, D_INNER, STATE), xdt.reshape(n, 1, D_INNER), dec.reshape(n, 1, D_INNER),
      bm.reshape(n, GROUPS, STATE), cm.reshape(n, GROUPS, STATE))

    x_new = pl.pallas_call(
        _ssd_sample_out_kernel,
        out_shape=jax.ShapeDtypeStruct((n, D_MODEL), F32),
        grid=(1,),
        in_specs=[_resident((n, D_MODEL)), _resident((n, D_INNER)), _resident((n, D_INNER)),
                  _resident((n, D_INNER)), _resident((1, D_INNER)), _resident((1, D_INNER)),
                  _resident((D_INNER, D_MODEL))],
        out_specs=full((n, D_MODEL)),
        compiler_params=_params(1),
        name="ssd_sample_out",
    )(x, y.reshape(n, D_INNER), xs, z, d_wide, norm_g, w_out)
    return (x_new, ssm_new.reshape(n, HEADS, HEAD_DIM, STATE),
            conv_new.reshape(n, CONV_K - 1, CONV_DIM))


def _pad_lanes(v):
    return jnp.pad(v.reshape(1, -1), ((0, 0), (0, LANES - v.shape[-1])))


def kernel(x_prompt, x_sample, state_ssm, state_conv, norm_mix_g, norm_mlp_g, norm_final_g,
           ssd_w_in, ssd_conv_w, ssd_conv_b, ssd_dt_bias, ssd_a_log, ssd_d, ssd_norm_g,
           ssd_w_out, cm_w_in, cm_b_in, cm_v_norm_g, cm_w_s, cm_b_s, cm_w_out,
           mlp_w_up, mlp_w_down):
    batch, seq, _ = x_prompt.shape
    dec_batch, dec_seq, _ = x_sample.shape
    assert dec_seq == 1 and seq % CHUNK == 0
    pos = PAST_LEN % CM_CHUNK

    xp = x_prompt.reshape(batch * seq, D_MODEL)
    xs = x_sample.reshape(dec_batch, D_MODEL)
    row = lambda v: v.reshape(1, -1)
    expand = jnp.asarray(np.kron(np.eye(LANES, HEADS, dtype=np.float32),
                                 np.ones((1, HEAD_DIM), np.float32)), BF16)

    ssm_p, conv_p, ssm_s, conv_s, v_s = [], [], [], [], []
    for i in range(DEPTH):
        j = i // 2
        g_mix = row(norm_mix_g[i])
        if i % 2 == 0:
            w_in = ssd_w_in[j].astype(BF16)
            w_z = w_in[:, :D_INNER]
            w_xbc = w_in[:, D_INNER:D_INNER + CONV_DIM]
            w_dt = jnp.pad(w_in[:, D_INNER + CONV_DIM:], ((0, 0), (0, LANES - HEADS)))
            w_out = ssd_w_out[j].astype(BF16)
            shared = (ssd_conv_w[j], row(ssd_conv_b[j]), _pad_lanes(ssd_dt_bias[j]),
                      _pad_lanes(ssd_a_log[j]))
            d_wide = jnp.repeat(ssd_d[j], HEAD_DIM).reshape(1, D_INNER)
            n_g = row(ssd_norm_g[j])
            xp, s_p, c_p = _ssd_prompt(xp, batch, g_mix, w_z, w_xbc, w_dt, *shared, d_wide, n_g, w_out)
            ssm_p.append(s_p.reshape(batch, HEADS, HEAD_DIM, STATE))
            conv_p.append(c_p)
            xs, s_s, c_s = _ssd_sample(xs, state_ssm, state_conv, j, g_mix, w_z, w_xbc, w_dt,
                                       *shared, expand, d_wide, n_g, w_out)
            ssm_s.append(s_s)
            conv_s.append(c_s)
        else:
            w_in = cm_w_in[j].astype(BF16)
            w_out = cm_w_out[j].astype(BF16)
            b_in = row(cm_b_in[j])
            v_g = row(cm_v_norm_g[j])
            b_s_wide = jnp.repeat(cm_b_s[j].T, CM_GROUP_DIM, axis=1)
            xp = _cm_prompt(xp, g_mix, w_in, b_in, v_g, cm_w_s[j], b_s_wide, w_out)
            w_diag = jnp.repeat(cm_w_s[j][:, pos, pos], CM_GROUP_DIM).reshape(1, CM_WIDTH)
            b_diag = jnp.repeat(cm_b_s[j][:, pos], CM_GROUP_DIM).reshape(1, CM_WIDTH)
            xs, v = _cm_sample(xs, g_mix, w_in, b_in, v_g, w_diag, b_diag, w_out)
            v_s.append(v.reshape(dec_batch, dec_seq, CM_WIDTH))
        final = i == DEPTH - 1
        g_mlp = row(norm_mlp_g[i])
        g_fin = row(norm_final_g)
        w_up = mlp_w_up[i].astype(BF16)
        w_down = mlp_w_down[i].astype(BF16)
        xp = _mlp(xp, g_mlp, w_up, w_down, g_fin, final)
        xs = _mlp(xs, g_mlp, w_up, w_down, g_fin, final)

    return (xp.reshape(batch, seq, D_MODEL), xs.reshape(dec_batch, dec_seq, D_MODEL),
            jnp.stack(ssm_p), jnp.stack(conv_p), jnp.stack(ssm_s), jnp.stack(conv_s),
            jnp.stack(v_s))
```

```python
import functools

import numpy as np
import jax
import jax.numpy as jnp
from jax import lax
from jax.experimental import pallas as pl
from jax.experimental.pallas import tpu as pltpu

F32 = jnp.float32
BF16 = jnp.bfloat16

D_MODEL = 1024
DEPTH = 4
D_INNER = 2048
HEAD_DIM = 64
HEADS = 32
GROUPS = 8
HEADS_PER_GROUP = 4
STATE = 128
CONV_K = 4
CHUNK = 128
CONV_DIM = D_INNER + 2 * GROUPS * STATE
GROUP_WIDTH = D_INNER // GROUPS
CM_CHUNK = 128
CM_WIDTH = 2 * D_MODEL
CM_GROUPS = 8
CM_GROUP_DIM = CM_WIDTH // CM_GROUPS
D_FF = 4 * D_MODEL
PAST_LEN = 16384
RMS_EPS = 1e-5

LANES = 128
SUBLANES = 8
CHUNK_TILES = CHUNK // SUBLANES
WRAP_ROWS = (CONV_K - 1) * SUBLANES
CHUNK_EXT = WRAP_ROWS + CHUNK
VMEM_LIMIT = 56 * 1024 * 1024
SSD_ROWS = 256
CM_ROWS = 256
MLP_ROWS = 512
FF_BLOCK = 1024
COL_BLOCK = 512
SAMPLE_BLOCK = 4
SQRT_HALF = np.sqrt(0.5).astype(np.float32)


def _rmsnorm(x, g):
    ms = jnp.mean(x * x, axis=-1, keepdims=True)
    return x * lax.rsqrt(ms + RMS_EPS) * g


def _gelu(x):
    return 0.5 * x * (1.0 + lax.erf(x * SQRT_HALF))


def _dot(a, b):
    return jnp.dot(a, b, preferred_element_type=F32)


def _resident(shape):
    nd = len(shape)
    return pl.BlockSpec(shape, lambda *_: (0,) * nd, pipeline_mode=pl.Buffered(1))


def _params(n_axes):
    return pltpu.CompilerParams(
        dimension_semantics=("arbitrary",) * n_axes, vmem_limit_bytes=VMEM_LIMIT)


def _mlp_kernel(x_ref, g_ref, wup_ref, wdown_ref, gf_ref, o_ref, *, final):
    x = x_ref[...]
    hn = _rmsnorm(x, g_ref[...]).astype(BF16)
    acc = x
    for c in range(D_FF // FF_BLOCK):
        cols = slice(c * FF_BLOCK, (c + 1) * FF_BLOCK)
        a = jnp.maximum(_dot(hn, wup_ref[:, cols]), 0.0)
        acc = acc + _dot((a * a).astype(BF16), wdown_ref[cols, :])
    if final:
        acc = _rmsnorm(acc, gf_ref[...])
    o_ref[...] = acc


def _mlp(x, g, w_up, w_down, g_final, final):
    n = x.shape[0]
    rows = min(MLP_ROWS, n)
    return pl.pallas_call(
        functools.partial(_mlp_kernel, final=final),
        out_shape=jax.ShapeDtypeStruct((n, D_MODEL), F32),
        grid=(n // rows,),
        in_specs=[pl.BlockSpec((rows, D_MODEL), lambda i: (i, 0)),
                  _resident((1, D_MODEL)),
                  _resident((D_MODEL, D_FF)),
                  _resident((D_FF, D_MODEL)),
                  _resident((1, D_MODEL))],
        out_specs=pl.BlockSpec((rows, D_MODEL), lambda i: (i, 0)),
        compiler_params=_params(1),
        name="mlp",
    )(x, g, w_up, w_down, g_final)


def _cm_prompt_kernel(x_ref, g_ref, win_ref, bin_ref, vg_ref, ws_ref, bs_ref, wout_ref,
                      o_ref, u_sc, v_sc, y_sc):
    rows = x_ref.shape[0]
    x = x_ref[...]
    hn = _rmsnorm(x, g_ref[...]).astype(BF16)
    for c in range(CM_WIDTH // COL_BLOCK):
        cols = slice(c * COL_BLOCK, (c + 1) * COL_BLOCK)
        u_sc[:, cols] = _gelu(_dot(hn, win_ref[:, cols]) + bin_ref[:, cols])
    ss = jnp.zeros((rows, 1), F32)
    for c in range(CM_WIDTH // COL_BLOCK):
        cols = slice(c * COL_BLOCK, (c + 1) * COL_BLOCK)
        wcols = slice(CM_WIDTH + c * COL_BLOCK, CM_WIDTH + (c + 1) * COL_BLOCK)
        gv = _gelu(_dot(hn, win_ref[:, wcols]) + bin_ref[:, wcols])
        ss = ss + jnp.sum(gv * gv, axis=-1, keepdims=True)
        v_sc[:, cols] = gv
    r = lax.rsqrt(ss * (1.0 / CM_WIDTH) + RMS_EPS)
    t_idx = lax.broadcasted_iota(jnp.int32, (CM_CHUNK, CM_CHUNK), 0)
    s_idx = lax.broadcasted_iota(jnp.int32, (CM_CHUNK, CM_CHUNK), 1)
    causal = t_idx >= s_idx
    for g in range(CM_GROUPS):
        cols = slice(g * CM_GROUP_DIM, (g + 1) * CM_GROUP_DIM)
        w = jnp.where(causal, ws_ref[g], 0.0).astype(BF16)
        for c in range(rows // CM_CHUNK):
            rs = slice(c * CM_CHUNK, (c + 1) * CM_CHUNK)
            v = (v_sc[rs, cols] * r[rs] * vg_ref[:, cols]).astype(BF16)
            s = _dot(w, v) + bs_ref[:, cols]
            y_sc[rs, cols] = (u_sc[rs, cols] * s).astype(BF16)
    o_ref[...] = x + _dot(y_sc[...], wout_ref[...])


def _cm_prompt(x, g, w_in, b_in, v_g, w_s, b_s_wide, w_out):
    n = x.shape[0]
    rows = min(CM_ROWS, n)
    return pl.pallas_call(
        _cm_prompt_kernel,
        out_shape=jax.ShapeDtypeStruct((n, D_MODEL), F32),
        grid=(n // rows,),
        in_specs=[pl.BlockSpec((rows, D_MODEL), lambda i: (i, 0)),
                  _resident((1, D_MODEL)),
                  _resident((D_MODEL, 2 * CM_WIDTH)),
                  _resident((1, 2 * CM_WIDTH)),
                  _resident((1, CM_WIDTH)),
                  _resident((CM_GROUPS, CM_CHUNK, CM_CHUNK)),
                  _resident((CM_CHUNK, CM_WIDTH)),
                  _resident((CM_WIDTH, D_MODEL))],
        out_specs=pl.BlockSpec((rows, D_MODEL), lambda i: (i, 0)),
        scratch_shapes=[pltpu.VMEM((rows, CM_WIDTH), F32),
                        pltpu.VMEM((rows, CM_WIDTH), F32),
                        pltpu.VMEM((rows, CM_WIDTH), BF16)],
        compiler_params=_params(1),
        name="cm_prompt",
    )(x, g, w_in, b_in, v_g, w_s, b_s_wide, w_out)


def _cm_sample_kernel(x_ref, g_ref, win_ref, bin_ref, vg_ref, wdiag_ref, bdiag_ref, wout_ref,
                      o_ref, v_ref):
    x = x_ref[...]
    hn = _rmsnorm(x, g_ref[...]).astype(BF16)
    u = _gelu(_dot(hn, win_ref[:, :CM_WIDTH]) + bin_ref[:, :CM_WIDTH])
    gv = _gelu(_dot(hn, win_ref[:, CM_WIDTH:]) + bin_ref[:, CM_WIDTH:])
    v = _rmsnorm(gv, vg_ref[...])
    v_ref[...] = v
    s = v * wdiag_ref[...] + bdiag_ref[...]
    o_ref[...] = x + _dot((u * s).astype(BF16), wout_ref[...])


def _cm_sample(x, g, w_in, b_in, v_g, w_diag, b_diag, w_out):
    n = x.shape[0]
    return pl.pallas_call(
        _cm_sample_kernel,
        out_shape=(jax.ShapeDtypeStruct((n, D_MODEL), F32),
                   jax.ShapeDtypeStruct((n, CM_WIDTH), F32)),
        grid=(1,),
        in_specs=[_resident((n, D_MODEL)),
                  _resident((1, D_MODEL)),
                  _resident((D_MODEL, 2 * CM_WIDTH)),
                  _resident((1, 2 * CM_WIDTH)),
                  _resident((1, CM_WIDTH)),
                  _resident((1, CM_WIDTH)),
                  _resident((1, CM_WIDTH)),
                  _resident((CM_WIDTH, D_MODEL))],
        out_specs=(pl.BlockSpec((n, D_MODEL), lambda i: (0, 0)),
                   pl.BlockSpec((n, CM_WIDTH), lambda i: (0, 0))),
        compiler_params=_params(1),
        name="cm_sample",
    )(x, g, w_in, b_in, v_g, w_diag, b_diag, w_out)


def _ssd_prompt_kernel(x_ref, g_ref, wz_ref, wxbc_ref, wdt_ref, convw_ref, convb_ref, dtb_ref,
                       alog_ref, dskip_ref, ng_ref, wout_ref,
                       o_ref, ssm_ref, convo_ref,
                       ext_sc, carry_sc, z_sc, act_sc, yn_sc, ht_sc):
    rows = x_ref.shape[0]
    n_chunks = rows // CHUNK
    step = pl.program_id(1)

    @pl.when(step == 0)
    def _():
        carry_sc[...] = jnp.zeros(carry_sc.shape, F32)
        ht_sc[...] = jnp.zeros(ht_sc.shape, F32)

    def token_of(r):
        return (r % SUBLANES) * CHUNK_TILES + (r % CHUNK) // SUBLANES

    r_idx = lax.broadcasted_iota(jnp.int32, (rows, rows), 0)
    c_idx = lax.broadcasted_iota(jnp.int32, (rows, rows), 1)
    same_chunk = (r_idx // CHUNK) == (c_idx // CHUNK)
    to_rows = (same_chunk & (c_idx % CHUNK == token_of(r_idx))).astype(BF16)
    to_tokens = (same_chunk & (r_idx % CHUNK == token_of(c_idx))).astype(BF16)

    x = x_ref[...]
    hn = _rmsnorm(x, g_ref[...]).astype(BF16)
    hn = _dot(to_rows, hn).astype(BF16)
    z_sc[...] = _dot(hn, wz_ref[...])
    xbc = _dot(hn, wxbc_ref[...])
    for c in range(n_chunks):
        ext_sc[c * CHUNK_EXT + WRAP_ROWS:(c + 1) * CHUNK_EXT, :] = xbc[c * CHUNK:(c + 1) * CHUNK]
    dt_raw = _dot(hn, wdt_ref[...])

    sub0 = lax.broadcasted_iota(jnp.int32, (WRAP_ROWS, COL_BLOCK), 0) % SUBLANES == 0
    for cblk in range(CONV_DIM // COL_BLOCK):
        cols = slice(cblk * COL_BLOCK, (cblk + 1) * COL_BLOCK)
        prev = carry_sc[:, cols]
        for c in range(n_chunks):
            base = c * CHUNK_EXT
            moved = jnp.concatenate(
                [pltpu.roll(ext_sc[base + CHUNK_EXT - WRAP_ROWS + m * SUBLANES:
                                   base + CHUNK_EXT - WRAP_ROWS + (m + 1) * SUBLANES, cols], 1, 0)
                 for m in range(CONV_K - 1)], axis=0)
            ext_sc[base:base + WRAP_ROWS, cols] = jnp.where(sub0, prev, moved)
            prev = moved
            conv = convb_ref[:, cols] + ext_sc[base:base + CHUNK, cols] * convw_ref[0:1, cols]
            for k in range(1, CONV_K):
                conv = conv + (ext_sc[base + k * SUBLANES:base + k * SUBLANES + CHUNK, cols]
                               * convw_ref[k:k + 1, cols])
            act_sc[c * CHUNK:(c + 1) * CHUNK, cols] = conv * jax.nn.sigmoid(conv)
        carry_sc[:, cols] = prev
    last = n_chunks * CHUNK_EXT
    for m in range(CONV_K - 1):
        r = last - (CONV_K - 1 - m) * SUBLANES + SUBLANES - 1
        convo_ref[0, m:m + 1, :] = ext_sc[r:r + 1, :]

    a_row = -jnp.exp(alog_ref[...])
    q_idx = lax.broadcasted_iota(jnp.int32, (CHUNK, CHUNK), 0)
    s_idx = lax.broadcasted_iota(jnp.int32, (CHUNK, CHUNK), 1)
    causal = token_of(q_idx) >= token_of(s_idx)
    tril = causal.astype(F32)
    low = s_idx < HEAD_DIM
    b_off = D_INNER
    c_off = D_INNER + GROUPS * STATE

    for c in range(n_chunks):
        rs = slice(c * CHUNK, (c + 1) * CHUNK)
        dt = jax.nn.softplus(dt_raw[rs] + dtb_ref[...])
        a = dt * a_row
        acum = jnp.dot(tril, a, precision=lax.Precision.HIGHEST, preferred_element_type=F32)
        acum_t = acum.T
        dt_t = dt.T
        w_t = jnp.exp(acum_t[:, CHUNK - 1:CHUNK] - acum_t) * dt_t
        cdec = jnp.exp(acum[CHUNK - 1:CHUNK, :])
        arow_t = acum_t - jnp.log(dt_t)
        for g in range(GROUPS):
            bg = act_sc[rs, b_off + g * STATE:b_off + (g + 1) * STATE]
            cg = act_sc[rs, c_off + g * STATE:c_off + (g + 1) * STATE].astype(BF16)
            bt = bg.T
            cb = _dot(cg, bt.astype(BF16))
            ht = ht_sc[g]
            y_off = _dot(cg, ht.astype(BF16))
            ys = []
            for pr in range(HEADS_PER_GROUP // 2):
                h0 = g * HEADS_PER_GROUP + 2 * pr
                lanes = slice(h0 * HEAD_DIM, (h0 + 2) * HEAD_DIM)
                half = slice(pr * LANES, (pr + 1) * LANES)
                xs = act_sc[rs, lanes]
                rhs = jnp.concatenate([jnp.where(low, xs, 0.0).astype(BF16),
                                       jnp.where(low, 0.0, xs).astype(BF16)], axis=0)
                ms, bs, cols_b = [], [], []
                for hd in (h0, h0 + 1):
                    col_b = jnp.broadcast_to(acum[:, hd:hd + 1], (CHUNK, CHUNK))
                    seg = col_b - arow_t[hd:hd + 1, :]
                    ms.append((cb * jnp.exp(jnp.where(causal, seg, -jnp.inf))).astype(BF16))
                    bs.append((bt * w_t[hd:hd + 1, :]).astype(BF16))
                    cols_b.append(col_b)
                lhs = jnp.concatenate([jnp.concatenate(ms, axis=1),
                                       jnp.concatenate(bs, axis=1)], axis=0)
                res = _dot(lhs, rhs)
                e = jnp.exp(jnp.where(low, cols_b[0], cols_b[1]))
                y = res[0:CHUNK] + e * y_off[:, half] + xs * dskip_ref[:, lanes]
                cd = jnp.where(low[0:1], cdec[:, h0:h0 + 1], cdec[:, h0 + 1:h0 + 2])
                ht_sc[g, :, half] = ht[:, half] * cd + res[CHUNK:2 * CHUNK]
                ys.append(y)
            gcols = slice(g * GROUP_WIDTH, (g + 1) * GROUP_WIDTH)
            yg = jnp.concatenate(ys, axis=1)
            zg = z_sc[rs, gcols]
            yg = yg * (zg * jax.nn.sigmoid(zg))
            yg = yg * lax.rsqrt(jnp.mean(yg * yg, axis=-1, keepdims=True) + RMS_EPS)
            yn_sc[rs, gcols] = (yg * ng_ref[:, gcols]).astype(BF16)

    yn = _dot(to_tokens, yn_sc[...]).astype(BF16)
    o_ref[...] = x + _dot(yn, wout_ref[...])

    @pl.when(step == pl.num_programs(1) - 1)
    def _():
        for g in range(GROUPS):
            ssm_ref[0, g * GROUP_WIDTH:(g + 1) * GROUP_WIDTH, :] = ht_sc[g].T


def _ssd_prompt(x, batch, g, w_z, w_xbc, w_dt, conv_w, conv_b, dt_bias, a_log, d_wide, norm_g, w_out):
    n = x.shape[0]
    seq = n // batch
    rows = min(SSD_ROWS, seq)
    steps = seq // rows
    return pl.pallas_call(
        _ssd_prompt_kernel,
        out_shape=(jax.ShapeDtypeStruct((n, D_MODEL), F32),
                   jax.ShapeDtypeStruct((batch, D_INNER, STATE), F32),
                   jax.ShapeDtypeStruct((batch, CONV_K - 1, CONV_DIM), F32)),
        grid=(batch, steps),
        in_specs=[pl.BlockSpec((rows, D_MODEL), lambda b, s: (b * steps + s, 0)),
                  _resident((1, D_MODEL)),
                  _resident((D_MODEL, D_INNER)),
                  _resident((D_MODEL, CONV_DIM)),
                  _resident((D_MODEL, LANES)),
                  _resident((CONV_K, CONV_DIM)),
                  _resident((1, CONV_DIM)),
                  _resident((1, LANES)),
                  _resident((1, LANES)),
                  _resident((1, D_INNER)),
                  _resident((1, D_INNER)),
                  _resident((D_INNER, D_MODEL))],
        out_specs=(pl.BlockSpec((rows, D_MODEL), lambda b, s: (b * steps + s, 0)),
                   pl.BlockSpec((1, D_INNER, STATE), lambda b, s: (b, 0, 0)),
                   pl.BlockSpec((1, CONV_K - 1, CONV_DIM), lambda b, s: (b, 0, 0))),
        scratch_shapes=[pltpu.VMEM((rows // CHUNK * CHUNK_EXT, CONV_DIM), F32),
                        pltpu.VMEM((WRAP_ROWS, CONV_DIM), F32),
                        pltpu.VMEM((rows, D_INNER), F32),
                        pltpu.VMEM((rows, CONV_DIM), F32),
                        pltpu.VMEM((rows, D_INNER), BF16),
                        pltpu.VMEM((GROUPS, STATE, GROUP_WIDTH), F32)],
        compiler_params=_params(2),
        name="ssd_prompt",
    )(x, g, w_z, w_xbc, w_dt, conv_w, conv_b, dt_bias, a_log, d_wide, norm_g, w_out)


def _ssd_sample_in_kernel(x_ref, g_ref, wz_ref, wxbc_ref, wdt_ref, cs_ref, convw_ref, convb_ref,
                          dtb_ref, alog_ref, expand_ref,
                          z_ref, xs_ref, b_ref, c_ref, xdt_ref, dec_ref, convo_ref):
    hn = _rmsnorm(x_ref[...], g_ref[...]).astype(BF16)
    z_ref[...] = _dot(hn, wz_ref[...])
    xbc = _dot(hn, wxbc_ref[...])
    dt_raw = _dot(hn, wdt_ref[...])
    conv = convb_ref[...] + cs_ref[:, 0:CONV_DIM] * convw_ref[0:1, :]
    for k in range(1, CONV_K - 1):
        conv = conv + cs_ref[:, k * CONV_DIM:(k + 1) * CONV_DIM] * convw_ref[k:k + 1, :]
    conv = conv + xbc * convw_ref[CONV_K - 1:CONV_K, :]
    convo_ref[:, 0:(CONV_K - 2) * CONV_DIM] = cs_ref[:, CONV_DIM:(CONV_K - 1) * CONV_DIM]
    convo_ref[:, (CONV_K - 2) * CONV_DIM:] = xbc
    act = conv * jax.nn.sigmoid(conv)
    xs = act[:, :D_INNER]
    xs_ref[...] = xs
    b_ref[...] = act[:, D_INNER:D_INNER + GROUPS * STATE]
    c_ref[...] = act[:, D_INNER + GROUPS * STATE:]
    dt = jax.nn.softplus(dt_raw + dtb_ref[...])
    dec = jnp.exp(dt * (-jnp.exp(alog_ref[...])))

    def widen(v):
        hi = v.astype(BF16)
        lo = (v - hi.astype(F32)).astype(BF16)
        return _dot(hi, expand_ref[...]) + _dot(lo, expand_ref[...])

    xdt_ref[...] = xs * widen(dt)
    dec_ref[...] = dec


def _ssd_sample_state_kernel(dec_ref, h_ref, xdt_ref, b_ref, c_ref, ho_ref, y_ref):
    phase = pl.program_id(0)

    @pl.when(phase != 0)
    def _():
        ho_ref[...] = h_ref[...]

    @pl.when(phase == 0)
    def _():
        base = pl.program_id(1) * SAMPLE_BLOCK
        g_idx = lax.broadcasted_iota(jnp.int32, (GROUPS, D_INNER), 0)
        l_idx = lax.broadcasted_iota(jnp.int32, (GROUPS, D_INNER), 1)
        own = (l_idx // GROUP_WIDTH) == g_idx
        for bi in range(SAMPLE_BLOCK):
            u_t = jnp.where(own, xdt_ref[bi], 0.0).astype(BF16)
            new = lax.dot_general(u_t, b_ref[bi].astype(BF16), (((0,), (0,)), ((), ())),
                                  preferred_element_type=F32)
            parts = []
            for hd in range(HEADS):
                rows = slice(hd * HEAD_DIM, (hd + 1) * HEAD_DIM)
                parts.append(h_ref[bi, rows, :] * dec_ref[base + bi, hd] + new[rows])
            h_new = jnp.concatenate(parts, axis=0)
            ho_ref[bi] = h_new
            y_all = lax.dot_general(c_ref[bi].astype(BF16), h_new.astype(BF16),
                                    (((1,), (1,)), ((), ())), preferred_element_type=F32)
            y_ref[bi] = jnp.sum(jnp.where(own, y_all, 0.0), axis=0, keepdims=True)


def _ssd_sample_out_kernel(x_ref, y_ref, xs_ref, z_ref, dskip_ref, ng_ref, wout_ref, o_ref):
    y = y_ref[...] + xs_ref[...] * dskip_ref[...]
    z = z_ref[...]
    y = y * (z * jax.nn.sigmoid(z))
    parts = []
    for g in range(GROUPS):
        yg = y[:, g * GROUP_WIDTH:(g + 1) * GROUP_WIDTH]
        parts.append(yg * lax.rsqrt(jnp.mean(yg * yg, axis=-1, keepdims=True) + RMS_EPS))
    yn = (jnp.concatenate(parts, axis=1) * ng_ref[...]).astype(BF16)
    o_ref[...] = x_ref[...] + _dot(yn, wout_ref[...])


def _ssd_sample(x, ssm_all, conv_all, ssm_new_all, layer, g, w_z, w_xbc, w_dt, conv_w, conv_b,
                dt_bias, a_log, expand, d_wide, norm_g, w_out):
    n = x.shape[0]
    n_layers = ssm_all.shape[0]
    gs = GROUPS * STATE
    full = lambda shape: pl.BlockSpec(shape, lambda i: (0,) * len(shape))
    z, xs, bm, cm, xdt, dec, conv_new = pl.pallas_call(
        _ssd_sample_in_kernel,
        out_shape=(jax.ShapeDtypeStruct((n, D_INNER), F32),
                   jax.ShapeDtypeStruct((n, D_INNER), F32),
                   jax.ShapeDtypeStruct((n, gs), F32),
                   jax.ShapeDtypeStruct((n, gs), F32),
                   jax.ShapeDtypeStruct((n, D_INNER), F32),
                   jax.ShapeDtypeStruct((n, LANES), F32),
                   jax.ShapeDtypeStruct((n, (CONV_K - 1) * CONV_DIM), F32)),
        grid=(1,),
        in_specs=[_resident((n, D_MODEL)),
                  _resident((1, D_MODEL)),
                  _resident((D_MODEL, D_INNER)),
                  _resident((D_MODEL, CONV_DIM)),
                  _resident((D_MODEL, LANES)),
                  pl.BlockSpec((n, (CONV_K - 1) * CONV_DIM), lambda i: (layer, 0),
                               pipeline_mode=pl.Buffered(1)),
                  _resident((CONV_K, CONV_DIM)),
                  _resident((1, CONV_DIM)),
                  _resident((1, LANES)),
                  _resident((1, LANES)),
                  _resident((LANES, D_INNER))],
        out_specs=(full((n, D_INNER)), full((n, D_INNER)), full((n, gs)), full((n, gs)),
                   full((n, D_INNER)), full((n, LANES)), full((n, (CONV_K - 1) * CONV_DIM))),
        compiler_params=_params(1),
        name="ssd_sample_in",
    )(x, g, w_z, w_xbc, w_dt, conv_all.reshape(n_layers * n, (CONV_K - 1) * CONV_DIM), conv_w, conv_b,
      dt_bias, a_log, expand)

    blk = SAMPLE_BLOCK
    steps = n // blk
    first = ssm_new_all is None
    phases = n_layers - layer if first else 1
    held = lambda p, i: (jnp.where(p == 0, i, steps - 1), 0, 0)
    row = pl.BlockSpec((blk, 1, D_INNER), held)
    grp = pl.BlockSpec((blk, GROUPS, STATE), held)
    state = pl.BlockSpec((blk, D_INNER, STATE), lambda p, i: ((layer + p) * steps + i, 0, 0))
    h_src = ssm_all.reshape(n_layers * n, D_INNER, STATE) if first else ssm_new_all
    ssm_new_all, y = pl.pallas_call(
        _ssd_sample_state_kernel,
        out_shape=(jax.ShapeDtypeStruct((n_layers * n, D_INNER, STATE), F32),
                   jax.ShapeDtypeStruct((n, 1, D_INNER), F32)),
        grid=(phases, steps),
        in_specs=[pl.BlockSpec(memory_space=pltpu.SMEM), state, row, grp, grp],
        out_specs=(state, row),
        input_output_aliases={} if first else {1: 0},
        compiler_params=_params(2),
        name="ssd_sample_state",
    )(dec[:, :HEADS], h_src, xdt.reshape(n, 1, D_INNER), bm.reshape(n, GROUPS, STATE),
      cm.reshape(n, GROUPS, STATE))

    x_new = pl.pallas_call(
        _ssd_sample_out_kernel,
        out_shape=jax.ShapeDtypeStruct((n, D_MODEL), F32),
        grid=(1,),
        in_specs=[_resident((n, D_MODEL)), _resident((n, D_INNER)), _resident((n, D_INNER)),
                  _resident((n, D_INNER)), _resident((1, D_INNER)), _resident((1, D_INNER)),
                  _resident((D_INNER, D_MODEL))],
        out_specs=full((n, D_MODEL)),
        compiler_params=_params(1),
        name="ssd_sample_out",
    )(x, y.reshape(n, D_INNER), xs, z, d_wide, norm_g, w_out)
    return x_new, ssm_new_all, conv_new.reshape(n, CONV_K - 1, CONV_DIM)


def _pad_lanes(v):
    return jnp.pad(v.reshape(1, -1), ((0, 0), (0, LANES - v.shape[-1])))


def kernel(x_prompt, x_sample, state_ssm, state_conv, norm_mix_g, norm_mlp_g, norm_final_g,
           ssd_w_in, ssd_conv_w, ssd_conv_b, ssd_dt_bias, ssd_a_log, ssd_d, ssd_norm_g,
           ssd_w_out, cm_w_in, cm_b_in, cm_v_norm_g, cm_w_s, cm_b_s, cm_w_out,
           mlp_w_up, mlp_w_down):
    batch, seq, _ = x_prompt.shape
    dec_batch, dec_seq, _ = x_sample.shape
    assert dec_seq == 1 and seq % CHUNK == 0
    pos = PAST_LEN % CM_CHUNK

    xp = x_prompt.reshape(batch * seq, D_MODEL)
    xs = x_sample.reshape(dec_batch, D_MODEL)
    row = lambda v: v.reshape(1, -1)
    expand = jnp.asarray(np.kron(np.eye(LANES, HEADS, dtype=np.float32),
                                 np.ones((1, HEAD_DIM), np.float32)), BF16)

    ssm_p, conv_p, conv_s, v_s = [], [], [], []
    ssm_s = None
    for i in range(DEPTH):
        j = i // 2
        g_mix = row(norm_mix_g[i])
        if i % 2 == 0:
            w_in = ssd_w_in[j].astype(BF16)
            w_z = w_in[:, :D_INNER]
            w_xbc = w_in[:, D_INNER:D_INNER + CONV_DIM]
            w_dt = jnp.pad(w_in[:, D_INNER + CONV_DIM:], ((0, 0), (0, LANES - HEADS)))
            w_out = ssd_w_out[j].astype(BF16)
            shared = (ssd_conv_w[j], row(ssd_conv_b[j]), _pad_lanes(ssd_dt_bias[j]),
                      _pad_lanes(ssd_a_log[j]))
            d_wide = jnp.repeat(ssd_d[j], HEAD_DIM).reshape(1, D_INNER)
            n_g = row(ssd_norm_g[j])
            xp, s_p, c_p = _ssd_prompt(xp, batch, g_mix, w_z, w_xbc, w_dt, *shared, d_wide, n_g, w_out)
            ssm_p.append(s_p.reshape(batch, HEADS, HEAD_DIM, STATE))
            conv_p.append(c_p)
            xs, ssm_s, c_s = _ssd_sample(xs, state_ssm, state_conv, ssm_s, j, g_mix, w_z, w_xbc, w_dt,
                                         *shared, expand, d_wide, n_g, w_out)
            conv_s.append(c_s)
        else:
            w_in = cm_w_in[j].astype(BF16)
            w_out = cm_w_out[j].astype(BF16)
            b_in = row(cm_b_in[j])
            v_g = row(cm_v_norm_g[j])
            b_s_wide = jnp.repeat(cm_b_s[j].T, CM_GROUP_DIM, axis=1)
            xp = _cm_prompt(xp, g_mix, w_in, b_in, v_g, cm_w_s[j], b_s_wide, w_out)
            w_diag = jnp.repeat(cm_w_s[j][:, pos, pos], CM_GROUP_DIM).reshape(1, CM_WIDTH)
            b_diag = jnp.repeat(cm_b_s[j][:, pos], CM_GROUP_DIM).reshape(1, CM_WIDTH)
            xs, v = _cm_sample(xs, g_mix, w_in, b_in, v_g, w_diag, b_diag, w_out)
            v_s.append(v.reshape(dec_batch, dec_seq, CM_WIDTH))
        final = i == DEPTH - 1
        g_mlp = row(norm_mlp_g[i])
        g_fin = row(norm_final_g)
        w_up = mlp_w_up[i].astype(BF16)
        w_down = mlp_w_down[i].astype(BF16)
        xp = _mlp(xp, g_mlp, w_up, w_down, g_fin, final)
        xs = _mlp(xs, g_mlp, w_up, w_down, g_fin, final)

    return (xp.reshape(batch, seq, D_MODEL), xs.reshape(dec_batch, dec_seq, D_MODEL),
            jnp.stack(ssm_p), jnp.stack(conv_p),
            ssm_s.reshape(state_ssm.shape), jnp.stack(conv_s), jnp.stack(v_s))
```

```python
import functools

import numpy as np
import jax
import jax.numpy as jnp
from jax import lax
from jax.experimental import pallas as pl
from jax.experimental.pallas import tpu as pltpu

F32 = jnp.float32
BF16 = jnp.bfloat16

D_MODEL = 1024
DEPTH = 4
D_INNER = 2048
HEAD_DIM = 64
HEADS = 32
GROUPS = 8
HEADS_PER_GROUP = 4
STATE = 128
CONV_K = 4
CHUNK = 128
CONV_DIM = D_INNER + 2 * GROUPS * STATE
GROUP_WIDTH = D_INNER // GROUPS
CM_CHUNK = 128
CM_WIDTH = 2 * D_MODEL
CM_GROUPS = 8
CM_GROUP_DIM = CM_WIDTH // CM_GROUPS
D_FF = 4 * D_MODEL
PAST_LEN = 16384
RMS_EPS = 1e-5

LANES = 128
SUBLANES = 8
CHUNK_TILES = CHUNK // SUBLANES
WRAP_ROWS = (CONV_K - 1) * SUBLANES
CHUNK_EXT = WRAP_ROWS + CHUNK
VMEM_LIMIT = 56 * 1024 * 1024
SSD_ROWS = 256
CM_ROWS = 256
MLP_ROWS = 512
FF_BLOCK = 1024
COL_BLOCK = 512
SAMPLE_BLOCK = 4
SQRT_HALF = np.sqrt(0.5).astype(np.float32)
Z_COLS = slice(0, D_INNER)
XBC_COLS = slice(D_INNER, D_INNER + CONV_DIM)
DT_COLS = slice(D_INNER + CONV_DIM, D_INNER + CONV_DIM + LANES)
SSD_IN_PAD = D_INNER + CONV_DIM + LANES


def _rmsnorm(x, g):
    ms = jnp.mean(x * x, axis=-1, keepdims=True)
    return x * lax.rsqrt(ms + RMS_EPS) * g


def _gelu(x):
    return 0.5 * x * (1.0 + lax.erf(x * SQRT_HALF))


def _dot(a, b):
    return jnp.dot(a, b, preferred_element_type=F32)


def _resident(shape):
    nd = len(shape)
    return pl.BlockSpec(shape, lambda *_: (0,) * nd, pipeline_mode=pl.Buffered(1))


def _layer_weight(shape, layer):
    nd = len(shape)
    return pl.BlockSpec((None,) + tuple(shape), lambda *_: (layer,) + (0,) * nd,
                        pipeline_mode=pl.Buffered(1))


def _params(n_axes):
    return pltpu.CompilerParams(
        dimension_semantics=("arbitrary",) * n_axes, vmem_limit_bytes=VMEM_LIMIT)


def _mlp_kernel(x_ref, g_ref, wup_ref, wdown_ref, gf_ref, o_ref, *, final):
    x = x_ref[...]
    hn = _rmsnorm(x, g_ref[...]).astype(BF16)
    acc = x
    for c in range(D_FF // FF_BLOCK):
        cols = slice(c * FF_BLOCK, (c + 1) * FF_BLOCK)
        a = jnp.maximum(_dot(hn, wup_ref[:, cols]), 0.0)
        acc = acc + _dot((a * a).astype(BF16), wdown_ref[cols, :])
    if final:
        acc = _rmsnorm(acc, gf_ref[...])
    o_ref[...] = acc


def _mlp(x, layer, g, w_up, w_down, g_final, final):
    n = x.shape[0]
    rows = min(MLP_ROWS, n)
    return pl.pallas_call(
        functools.partial(_mlp_kernel, final=final),
        out_shape=jax.ShapeDtypeStruct((n, D_MODEL), F32),
        grid=(n // rows,),
        in_specs=[pl.BlockSpec((rows, D_MODEL), lambda i: (i, 0)),
                  _resident((1, D_MODEL)),
                  _layer_weight((D_MODEL, D_FF), layer),
                  _layer_weight((D_FF, D_MODEL), layer),
                  _resident((1, D_MODEL))],
        out_specs=pl.BlockSpec((rows, D_MODEL), lambda i: (i, 0)),
        compiler_params=_params(1),
        name="mlp",
    )(x, g, w_up, w_down, g_final)


def _cm_prompt_kernel(x_ref, g_ref, win_ref, bin_ref, vg_ref, ws_ref, bs_ref, wout_ref,
                      o_ref, u_sc, v_sc, y_sc):
    rows = x_ref.shape[0]
    x = x_ref[...]
    hn = _rmsnorm(x, g_ref[...]).astype(BF16)
    for c in range(CM_WIDTH // COL_BLOCK):
        cols = slice(c * COL_BLOCK, (c + 1) * COL_BLOCK)
        u_sc[:, cols] = _gelu(_dot(hn, win_ref[:, cols]) + bin_ref[:, cols])
    ss = jnp.zeros((rows, 1), F32)
    for c in range(CM_WIDTH // COL_BLOCK):
        cols = slice(c * COL_BLOCK, (c + 1) * COL_BLOCK)
        wcols = slice(CM_WIDTH + c * COL_BLOCK, CM_WIDTH + (c + 1) * COL_BLOCK)
        gv = _gelu(_dot(hn, win_ref[:, wcols]) + bin_ref[:, wcols])
        ss = ss + jnp.sum(gv * gv, axis=-1, keepdims=True)
        v_sc[:, cols] = gv
    r = lax.rsqrt(ss * (1.0 / CM_WIDTH) + RMS_EPS)
    t_idx = lax.broadcasted_iota(jnp.int32, (CM_CHUNK, CM_CHUNK), 0)
    s_idx = lax.broadcasted_iota(jnp.int32, (CM_CHUNK, CM_CHUNK), 1)
    causal = t_idx >= s_idx
    for g in range(CM_GROUPS):
        cols = slice(g * CM_GROUP_DIM, (g + 1) * CM_GROUP_DIM)
        w = jnp.where(causal, ws_ref[g], 0.0).astype(BF16)
        for c in range(rows // CM_CHUNK):
            rs = slice(c * CM_CHUNK, (c + 1) * CM_CHUNK)
            v = (v_sc[rs, cols] * r[rs] * vg_ref[:, cols]).astype(BF16)
            s = _dot(w, v) + bs_ref[:, cols]
            y_sc[rs, cols] = (u_sc[rs, cols] * s).astype(BF16)
    o_ref[...] = x + _dot(y_sc[...], wout_ref[...])


def _cm_prompt(x, layer, g, w_in, b_in, v_g, w_s, b_s_wide, w_out):
    n = x.shape[0]
    rows = min(CM_ROWS, n)
    return pl.pallas_call(
        _cm_prompt_kernel,
        out_shape=jax.ShapeDtypeStruct((n, D_MODEL), F32),
        grid=(n // rows,),
        in_specs=[pl.BlockSpec((rows, D_MODEL), lambda i: (i, 0)),
                  _resident((1, D_MODEL)),
                  _layer_weight((D_MODEL, 2 * CM_WIDTH), layer),
                  _resident((1, 2 * CM_WIDTH)),
                  _resident((1, CM_WIDTH)),
                  _layer_weight((CM_GROUPS, CM_CHUNK, CM_CHUNK), layer),
                  _resident((CM_CHUNK, CM_WIDTH)),
                  _layer_weight((CM_WIDTH, D_MODEL), layer)],
        out_specs=pl.BlockSpec((rows, D_MODEL), lambda i: (i, 0)),
        scratch_shapes=[pltpu.VMEM((rows, CM_WIDTH), F32),
                        pltpu.VMEM((rows, CM_WIDTH), F32),
                        pltpu.VMEM((rows, CM_WIDTH), BF16)],
        compiler_params=_params(1),
        name="cm_prompt",
    )(x, g, w_in, b_in, v_g, w_s, b_s_wide, w_out)


def _cm_sample_kernel(x_ref, g_ref, win_ref, bin_ref, vg_ref, wdiag_ref, bdiag_ref, wout_ref,
                      o_ref, v_ref):
    x = x_ref[...]
    hn = _rmsnorm(x, g_ref[...]).astype(BF16)
    u = _gelu(_dot(hn, win_ref[:, :CM_WIDTH]) + bin_ref[:, :CM_WIDTH])
    gv = _gelu(_dot(hn, win_ref[:, CM_WIDTH:]) + bin_ref[:, CM_WIDTH:])
    v = _rmsnorm(gv, vg_ref[...])
    v_ref[...] = v
    s = v * wdiag_ref[...] + bdiag_ref[...]
    o_ref[...] = x + _dot((u * s).astype(BF16), wout_ref[...])


def _cm_sample(x, layer, g, w_in, b_in, v_g, w_diag, b_diag, w_out):
    n = x.shape[0]
    return pl.pallas_call(
        _cm_sample_kernel,
        out_shape=(jax.ShapeDtypeStruct((n, D_MODEL), F32),
                   jax.ShapeDtypeStruct((n, CM_WIDTH), F32)),
        grid=(1,),
        in_specs=[_resident((n, D_MODEL)),
                  _resident((1, D_MODEL)),
                  _layer_weight((D_MODEL, 2 * CM_WIDTH), layer),
                  _resident((1, 2 * CM_WIDTH)),
                  _resident((1, CM_WIDTH)),
                  _resident((1, CM_WIDTH)),
                  _resident((1, CM_WIDTH)),
                  _layer_weight((CM_WIDTH, D_MODEL), layer)],
        out_specs=(pl.BlockSpec((n, D_MODEL), lambda i: (0, 0)),
                   pl.BlockSpec((n, CM_WIDTH), lambda i: (0, 0))),
        compiler_params=_params(1),
        name="cm_sample",
    )(x, g, w_in, b_in, v_g, w_diag, b_diag, w_out)


def _ssd_prompt_kernel(x_ref, g_ref, win_ref, convw_ref, convb_ref, dtb_ref,
                       alog_ref, dskip_ref, ng_ref, wout_ref,
                       o_ref, ssm_ref, convo_ref,
                       ext_sc, carry_sc, z_sc, dt_sc, act_sc, yn_sc, ht_sc):
    rows = x_ref.shape[0]
    n_chunks = rows // CHUNK
    step = pl.program_id(1)

    @pl.when(step == 0)
    def _():
        carry_sc[...] = jnp.zeros(carry_sc.shape, F32)
        ht_sc[...] = jnp.zeros(ht_sc.shape, F32)

    def token_of(r):
        return (r % SUBLANES) * CHUNK_TILES + r // SUBLANES

    q_idx = lax.broadcasted_iota(jnp.int32, (CHUNK, CHUNK), 0)
    s_idx = lax.broadcasted_iota(jnp.int32, (CHUNK, CHUNK), 1)
    to_rows = (s_idx == token_of(q_idx)).astype(BF16)
    to_tokens = (q_idx == token_of(s_idx)).astype(BF16)
    causal = token_of(q_idx) >= token_of(s_idx)
    tril = causal.astype(F32)
    low = s_idx < HEAD_DIM

    for c in range(n_chunks):
        rs = slice(c * CHUNK, (c + 1) * CHUNK)
        hn = _rmsnorm(x_ref[rs, :], g_ref[...]).astype(BF16)
        hn = _dot(to_rows, hn).astype(BF16)
        z_sc[rs, :] = _dot(hn, win_ref[:, Z_COLS])
        dt_sc[rs, :] = _dot(hn, win_ref[:, DT_COLS])
        ext_sc[c * CHUNK_EXT + WRAP_ROWS:(c + 1) * CHUNK_EXT, :] = _dot(hn, win_ref[:, XBC_COLS])

    sub0 = lax.broadcasted_iota(jnp.int32, (WRAP_ROWS, COL_BLOCK), 0) % SUBLANES == 0
    for c in range(n_chunks):
        base = c * CHUNK_EXT
        for cblk in range(CONV_DIM // COL_BLOCK):
            cols = slice(cblk * COL_BLOCK, (cblk + 1) * COL_BLOCK)
            moved = jnp.concatenate(
                [pltpu.roll(ext_sc[base + CHUNK_EXT - WRAP_ROWS + m * SUBLANES:
                                   base + CHUNK_EXT - WRAP_ROWS + (m + 1) * SUBLANES, cols], 1, 0)
                 for m in range(CONV_K - 1)], axis=0)
            ext_sc[base:base + WRAP_ROWS, cols] = jnp.where(sub0, carry_sc[:, cols], moved)
            carry_sc[:, cols] = moved
            conv = convb_ref[:, cols] + ext_sc[base:base + CHUNK, cols] * convw_ref[0:1, cols]
            for k in range(1, CONV_K):
                conv = conv + (ext_sc[base + k * SUBLANES:base + k * SUBLANES + CHUNK, cols]
                               * convw_ref[k:k + 1, cols])
            act_sc[c * CHUNK:(c + 1) * CHUNK, cols] = conv * jax.nn.sigmoid(conv)
    last = n_chunks * CHUNK_EXT
    for m in range(CONV_K - 1):
        r = last - (CONV_K - 1 - m) * SUBLANES + SUBLANES - 1
        convo_ref[0, m:m + 1, :] = ext_sc[r:r + 1, :]

    a_row = -jnp.exp(alog_ref[...])
    b_off = D_INNER
    c_off = D_INNER + GROUPS * STATE

    for c in range(n_chunks):
        rs = slice(c * CHUNK, (c + 1) * CHUNK)
        dt = jax.nn.softplus(dt_sc[rs, :] + dtb_ref[...])
        a = dt * a_row
        acum = jnp.dot(tril, a, precision=lax.Precision.HIGHEST, preferred_element_type=F32)
        acum_t = acum.T
        dt_t = dt.T
        w_t = jnp.exp(acum_t[:, CHUNK - 1:CHUNK] - acum_t) * dt_t
        cdec = jnp.exp(acum[CHUNK - 1:CHUNK, :])
        arow_t = acum_t - jnp.log(dt_t)
        for g in range(GROUPS):
            bg = act_sc[rs, b_off + g * STATE:b_off + (g + 1) * STATE]
            cg = act_sc[rs, c_off + g * STATE:c_off + (g + 1) * STATE].astype(BF16)
            bt = bg.T
            cb = _dot(cg, bt.astype(BF16))
            ht = ht_sc[g]
            y_off = _dot(cg, ht.astype(BF16))
            ys = []
            for pr in range(HEADS_PER_GROUP // 2):
                h0 = g * HEADS_PER_GROUP + 2 * pr
                lanes = slice(h0 * HEAD_DIM, (h0 + 2) * HEAD_DIM)
                half = slice(pr * LANES, (pr + 1) * LANES)
                xs = act_sc[rs, lanes]
                rhs = jnp.concatenate([jnp.where(low, xs, 0.0).astype(BF16),
                                       jnp.where(low, 0.0, xs).astype(BF16)], axis=0)
                ms, bs, cols_b = [], [], []
                for hd in (h0, h0 + 1):
                    col_b = jnp.broadcast_to(acum[:, hd:hd + 1], (CHUNK, CHUNK))
                    seg = col_b - arow_t[hd:hd + 1, :]
                    ms.append((cb * jnp.exp(jnp.where(causal, seg, -jnp.inf))).astype(BF16))
                    bs.append((bt * w_t[hd:hd + 1, :]).astype(BF16))
                    cols_b.append(col_b)
                lhs = jnp.concatenate([jnp.concatenate(ms, axis=1),
                                       jnp.concatenate(bs, axis=1)], axis=0)
                res = _dot(lhs, rhs)
                e = jnp.exp(jnp.where(low, cols_b[0], cols_b[1]))
                y = res[0:CHUNK] + e * y_off[:, half] + xs * dskip_ref[:, lanes]
                cd = jnp.where(low[0:1], cdec[:, h0:h0 + 1], cdec[:, h0 + 1:h0 + 2])
                ht_sc[g, :, half] = ht[:, half] * cd + res[CHUNK:2 * CHUNK]
                ys.append(y)
            gcols = slice(g * GROUP_WIDTH, (g + 1) * GROUP_WIDTH)
            yg = jnp.concatenate(ys, axis=1)
            zg = z_sc[rs, gcols]
            yg = yg * (zg * jax.nn.sigmoid(zg))
            yg = yg * lax.rsqrt(jnp.mean(yg * yg, axis=-1, keepdims=True) + RMS_EPS)
            yn_sc[rs, gcols] = (yg * ng_ref[:, gcols]).astype(BF16)
        yn = _dot(to_tokens, yn_sc[rs, :]).astype(BF16)
        o_ref[rs, :] = x_ref[rs, :] + _dot(yn, wout_ref[...])

    @pl.when(step == pl.num_programs(1) - 1)
    def _():
        for g in range(GROUPS):
            ssm_ref[0, g * GROUP_WIDTH:(g + 1) * GROUP_WIDTH, :] = ht_sc[g].T


def _ssd_prompt(x, batch, layer, g, w_in, conv_w, conv_b, dt_bias, a_log, d_wide, norm_g, w_out):
    n = x.shape[0]
    seq = n // batch
    rows = min(SSD_ROWS, seq)
    steps = seq // rows
    return pl.pallas_call(
        _ssd_prompt_kernel,
        out_shape=(jax.ShapeDtypeStruct((n, D_MODEL), F32),
                   jax.ShapeDtypeStruct((batch, D_INNER, STATE), F32),
                   jax.ShapeDtypeStruct((batch, CONV_K - 1, CONV_DIM), F32)),
        grid=(batch, steps),
        in_specs=[pl.BlockSpec((rows, D_MODEL), lambda b, s: (b * steps + s, 0)),
                  _resident((1, D_MODEL)),
                  _layer_weight((D_MODEL, SSD_IN_PAD), layer),
                  _resident((CONV_K, CONV_DIM)),
                  _resident((1, CONV_DIM)),
                  _resident((1, LANES)),
                  _resident((1, LANES)),
                  _resident((1, D_INNER)),
                  _resident((1, D_INNER)),
                  _layer_weight((D_INNER, D_MODEL), layer)],
        out_specs=(pl.BlockSpec((rows, D_MODEL), lambda b, s: (b * steps + s, 0)),
                   pl.BlockSpec((1, D_INNER, STATE), lambda b, s: (b, 0, 0)),
                   pl.BlockSpec((1, CONV_K - 1, CONV_DIM), lambda b, s: (b, 0, 0))),
        scratch_shapes=[pltpu.VMEM((rows // CHUNK * CHUNK_EXT, CONV_DIM), F32),
                        pltpu.VMEM((WRAP_ROWS, CONV_DIM), F32),
                        pltpu.VMEM((rows, D_INNER), F32),
                        pltpu.VMEM((rows, LANES), F32),
                        pltpu.VMEM((rows, CONV_DIM), F32),
                        pltpu.VMEM((rows, D_INNER), BF16),
                        pltpu.VMEM((GROUPS, STATE, GROUP_WIDTH), F32)],
        compiler_params=_params(2),
        name="ssd_prompt",
    )(x, g, w_in, conv_w, conv_b, dt_bias, a_log, d_wide, norm_g, w_out)


def _ssd_sample_in_kernel(x_ref, g_ref, win_ref, cs_ref, convw_ref, convb_ref,
                          dtb_ref, alog_ref, expand_ref,
                          z_ref, xs_ref, b_ref, c_ref, xdt_ref, dec_ref, convo_ref):
    hn = _rmsnorm(x_ref[...], g_ref[...]).astype(BF16)
    z_ref[...] = _dot(hn, win_ref[:, Z_COLS])
    xbc = _dot(hn, win_ref[:, XBC_COLS])
    dt_raw = _dot(hn, win_ref[:, DT_COLS])
    conv = convb_ref[...] + cs_ref[:, 0:CONV_DIM] * convw_ref[0:1, :]
    for k in range(1, CONV_K - 1):
        conv = conv + cs_ref[:, k * CONV_DIM:(k + 1) * CONV_DIM] * convw_ref[k:k + 1, :]
    conv = conv + xbc * convw_ref[CONV_K - 1:CONV_K, :]
    convo_ref[:, 0:(CONV_K - 2) * CONV_DIM] = cs_ref[:, CONV_DIM:(CONV_K - 1) * CONV_DIM]
    convo_ref[:, (CONV_K - 2) * CONV_DIM:] = xbc
    act = conv * jax.nn.sigmoid(conv)
    xs = act[:, :D_INNER]
    xs_ref[...] = xs
    b_ref[...] = act[:, D_INNER:D_INNER + GROUPS * STATE]
    c_ref[...] = act[:, D_INNER + GROUPS * STATE:]
    dt = jax.nn.softplus(dt_raw + dtb_ref[...])
    dec = jnp.exp(dt * (-jnp.exp(alog_ref[...])))

    def widen(v):
        hi = v.astype(BF16)
        lo = (v - hi.astype(F32)).astype(BF16)
        return _dot(hi, expand_ref[...]) + _dot(lo, expand_ref[...])

    xdt_ref[...] = xs * widen(dt)
    dec_ref[...] = dec


def _ssd_sample_state_kernel(dec_ref, h_ref, xdt_ref, b_ref, c_ref, stacked_ref, ho_ref, y_ref):
    del stacked_ref
    phase = pl.program_id(0)

    @pl.when(phase != 0)
    def _():
        ho_ref[...] = jnp.zeros(ho_ref.shape, F32)

    @pl.when(phase == 0)
    def _():
        base = pl.program_id(1) * SAMPLE_BLOCK
        g_idx = lax.broadcasted_iota(jnp.int32, (GROUPS, D_INNER), 0)
        l_idx = lax.broadcasted_iota(jnp.int32, (GROUPS, D_INNER), 1)
        own = (l_idx // GROUP_WIDTH) == g_idx
        for bi in range(SAMPLE_BLOCK):
            u_t = jnp.where(own, xdt_ref[bi], 0.0).astype(BF16)
            new = lax.dot_general(u_t, b_ref[bi].astype(BF16), (((0,), (0,)), ((), ())),
                                  preferred_element_type=F32)
            parts = []
            for hd in range(HEADS):
                rows = slice(hd * HEAD_DIM, (hd + 1) * HEAD_DIM)
                parts.append(h_ref[bi, rows, :] * dec_ref[base + bi, hd] + new[rows])
            h_new = jnp.concatenate(parts, axis=0)
            ho_ref[bi] = h_new
            y_all = lax.dot_general(c_ref[bi].astype(BF16), h_new.astype(BF16),
                                    (((1,), (1,)), ((), ())), preferred_element_type=F32)
            y_ref[bi] = jnp.sum(jnp.where(own, y_all, 0.0), axis=0, keepdims=True)


def _ssd_sample_out_kernel(x_ref, y_ref, xs_ref, z_ref, dskip_ref, ng_ref, wout_ref, o_ref):
    y = y_ref[...] + xs_ref[...] * dskip_ref[...]
    z = z_ref[...]
    y = y * (z * jax.nn.sigmoid(z))
    parts = []
    for g in range(GROUPS):
        yg = y[:, g * GROUP_WIDTH:(g + 1) * GROUP_WIDTH]
        parts.append(yg * lax.rsqrt(jnp.mean(yg * yg, axis=-1, keepdims=True) + RMS_EPS))
    yn = (jnp.concatenate(parts, axis=1) * ng_ref[...]).astype(BF16)
    o_ref[...] = x_ref[...] + _dot(yn, wout_ref[...])


def _ssd_sample(x, ssm_all, conv_all, ssm_new_all, layer, g, w_in, conv_w, conv_b,
                dt_bias, a_log, expand, d_wide, norm_g, w_out):
    n = x.shape[0]
    n_layers = ssm_all.shape[0]
    gs = GROUPS * STATE
    full = lambda shape: pl.BlockSpec(shape, lambda i: (0,) * len(shape))
    z, xs, bm, cm, xdt, dec, conv_new = pl.pallas_call(
        _ssd_sample_in_kernel,
        out_shape=(jax.ShapeDtypeStruct((n, D_INNER), F32),
                   jax.ShapeDtypeStruct((n, D_INNER), F32),
                   jax.ShapeDtypeStruct((n, gs), F32),
                   jax.ShapeDtypeStruct((n, gs), F32),
                   jax.ShapeDtypeStruct((n, D_INNER), F32),
                   jax.ShapeDtypeStruct((n, LANES), F32),
                   jax.ShapeDtypeStruct((n, (CONV_K - 1) * CONV_DIM), F32)),
        grid=(1,),
        in_specs=[_resident((n, D_MODEL)),
                  _resident((1, D_MODEL)),
                  _layer_weight((D_MODEL, SSD_IN_PAD), layer),
                  pl.BlockSpec((n, (CONV_K - 1) * CONV_DIM), lambda i: (layer, 0),
                               pipeline_mode=pl.Buffered(1)),
                  _resident((CONV_K, CONV_DIM)),
                  _resident((1, CONV_DIM)),
                  _resident((1, LANES)),
                  _resident((1, LANES)),
                  _resident((LANES, D_INNER))],
        out_specs=(full((n, D_INNER)), full((n, D_INNER)), full((n, gs)), full((n, gs)),
                   full((n, D_INNER)), full((n, LANES)), full((n, (CONV_K - 1) * CONV_DIM))),
        compiler_params=_params(1),
        name="ssd_sample_in",
    )(x, g, w_in, conv_all.reshape(n_layers * n, (CONV_K - 1) * CONV_DIM), conv_w, conv_b,
      dt_bias, a_log, expand)

    blk = SAMPLE_BLOCK
    steps = n // blk
    first = ssm_new_all is None
    phases = n_layers - layer if first else 1
    held = lambda p, i: (jnp.where(p == 0, i, steps - 1), 0, 0)
    row = pl.BlockSpec((blk, 1, D_INNER), held)
    grp = pl.BlockSpec((blk, GROUPS, STATE), held)
    h_old = pl.BlockSpec((blk, D_INNER, STATE),
                         lambda p, i: (layer * steps + jnp.where(p == 0, i, steps - 1), 0, 0))
    h_new = pl.BlockSpec((blk, D_INNER, STATE), lambda p, i: ((layer + p) * steps + i, 0, 0))
    carried = ssm_all.reshape(n_layers * n, D_INNER, STATE)
    ssm_new_all, y = pl.pallas_call(
        _ssd_sample_state_kernel,
        out_shape=(jax.ShapeDtypeStruct((n_layers * n, D_INNER, STATE), F32),
                   jax.ShapeDtypeStruct((n, 1, D_INNER), F32)),
        grid=(phases, steps),
        in_specs=[pl.BlockSpec(memory_space=pltpu.SMEM), h_old, row, grp, grp,
                  pl.BlockSpec(memory_space=pl.ANY)],
        out_specs=(h_new, row),
        input_output_aliases={} if first else {5: 0},
        compiler_params=_params(2),
        name="ssd_sample_state",
    )(dec[:, :HEADS], carried, xdt.reshape(n, 1, D_INNER), bm.reshape(n, GROUPS, STATE),
      cm.reshape(n, GROUPS, STATE), carried if first else ssm_new_all)

    x_new = pl.pallas_call(
        _ssd_sample_out_kernel,
        out_shape=jax.ShapeDtypeStruct((n, D_MODEL), F32),
        grid=(1,),
        in_specs=[_resident((n, D_MODEL)), _resident((n, D_INNER)), _resident((n, D_INNER)),
                  _resident((n, D_INNER)), _resident((1, D_INNER)), _resident((1, D_INNER)),
                  _layer_weight((D_INNER, D_MODEL), layer)],
        out_specs=full((n, D_MODEL)),
        compiler_params=_params(1),
        name="ssd_sample_out",
    )(x, y.reshape(n, D_INNER), xs, z, d_wide, norm_g, w_out)
    return x_new, ssm_new_all, conv_new.reshape(n, CONV_K - 1, CONV_DIM)


def _pad_lanes(v):
    return jnp.pad(v.reshape(1, -1), ((0, 0), (0, LANES - v.shape[-1])))


def kernel(x_prompt, x_sample, state_ssm, state_conv, norm_mix_g, norm_mlp_g, norm_final_g,
           ssd_w_in, ssd_conv_w, ssd_conv_b, ssd_dt_bias, ssd_a_log, ssd_d, ssd_norm_g,
           ssd_w_out, cm_w_in, cm_b_in, cm_v_norm_g, cm_w_s, cm_b_s, cm_w_out,
           mlp_w_up, mlp_w_down):
    batch, seq, _ = x_prompt.shape
    dec_batch, dec_seq, _ = x_sample.shape
    assert dec_seq == 1 and seq % CHUNK == 0
    pos = PAST_LEN % CM_CHUNK

    xp = x_prompt.reshape(batch * seq, D_MODEL)
    xs = x_sample.reshape(dec_batch, D_MODEL)
    row = lambda v: v.reshape(1, -1)
    expand = jnp.asarray(np.kron(np.eye(LANES, HEADS, dtype=np.float32),
                                 np.ones((1, HEAD_DIM), np.float32)), BF16)

    w_ssd_in = jnp.pad(ssd_w_in.astype(BF16), ((0, 0), (0, 0), (0, LANES - HEADS)))
    w_ssd_out = ssd_w_out.astype(BF16)
    w_cm_in = cm_w_in.astype(BF16)
    w_cm_out = cm_w_out.astype(BF16)
    w_up = mlp_w_up.astype(BF16)
    w_down = mlp_w_down.astype(BF16)

    ssm_p, conv_p, conv_s, v_s = [], [], [], []
    ssm_s = None
    for i in range(DEPTH):
        j = i // 2
        g_mix = row(norm_mix_g[i])
        if i % 2 == 0:
            shared = (ssd_conv_w[j], row(ssd_conv_b[j]), _pad_lanes(ssd_dt_bias[j]),
                      _pad_lanes(ssd_a_log[j]))
            d_wide = jnp.repeat(ssd_d[j], HEAD_DIM).reshape(1, D_INNER)
            n_g = row(ssd_norm_g[j])
            xp, s_p, c_p = _ssd_prompt(xp, batch, j, g_mix, w_ssd_in, *shared, d_wide, n_g, w_ssd_out)
            ssm_p.append(s_p.reshape(batch, HEADS, HEAD_DIM, STATE))
            conv_p.append(c_p)
            xs, ssm_s, c_s = _ssd_sample(xs, state_ssm, state_conv, ssm_s, j, g_mix, w_ssd_in,
                                         *shared, expand, d_wide, n_g, w_ssd_out)
            conv_s.append(c_s)
        else:
            b_in = row(cm_b_in[j])
            v_g = row(cm_v_norm_g[j])
            b_s_wide = jnp.repeat(cm_b_s[j].T, CM_GROUP_DIM, axis=1)
            xp = _cm_prompt(xp, j, g_mix, w_cm_in, b_in, v_g, cm_w_s, b_s_wide, w_cm_out)
            w_diag = jnp.repeat(cm_w_s[j][:, pos, pos], CM_GROUP_DIM).reshape(1, CM_WIDTH)
            b_diag = jnp.repeat(cm_b_s[j][:, pos], CM_GROUP_DIM).reshape(1, CM_WIDTH)
            xs, v = _cm_sample(xs, j, g_mix, w_cm_in, b_in, v_g, w_diag, b_diag, w_cm_out)
            v_s.append(v.reshape(dec_batch, dec_seq, CM_WIDTH))
        final = i == DEPTH - 1
        g_mlp = row(norm_mlp_g[i])
        g_fin = row(norm_final_g)
        xp = _mlp(xp, i, g_mlp, w_up, w_down, g_fin, final)
        xs = _mlp(xs, i, g_mlp, w_up, w_down, g_fin, final)

    return (xp.reshape(batch, seq, D_MODEL), xs.reshape(dec_batch, dec_seq, D_MODEL),
            jnp.stack(ssm_p), jnp.stack(conv_p),
            ssm_s.reshape(state_ssm.shape), jnp.stack(conv_s), jnp.stack(v_s))
```

```python
import functools

import numpy as np
import jax
import jax.numpy as jnp
from jax import lax
from jax.experimental import pallas as pl
from jax.experimental.pallas import tpu as pltpu

F32 = jnp.float32
BF16 = jnp.bfloat16

D_MODEL = 1024
DEPTH = 4
D_INNER = 2048
HEAD_DIM = 64
HEADS = 32
GROUPS = 8
HEADS_PER_GROUP = 4
STATE = 128
CONV_K = 4
CHUNK = 128
CONV_DIM = D_INNER + 2 * GROUPS * STATE
GROUP_WIDTH = D_INNER // GROUPS
CM_CHUNK = 128
CM_WIDTH = 2 * D_MODEL
CM_GROUPS = 8
CM_GROUP_DIM = CM_WIDTH // CM_GROUPS
D_FF = 4 * D_MODEL
PAST_LEN = 16384
RMS_EPS = 1e-5

LANES = 128
SUBLANES = 8
CHUNK_TILES = CHUNK // SUBLANES
WRAP_ROWS = (CONV_K - 1) * SUBLANES
CHUNK_EXT = WRAP_ROWS + CHUNK
VMEM_LIMIT = 56 * 1024 * 1024
SSD_ROWS = 256
CM_ROWS = 256
MLP_ROWS = 512
FF_BLOCK = 1024
COL_BLOCK = 512
SAMPLE_BLOCK = 4
SQRT_HALF = np.sqrt(0.5).astype(np.float32)
Z_COLS = slice(0, D_INNER)
XBC_COLS = slice(D_INNER, D_INNER + CONV_DIM)
DT_COLS = slice(D_INNER + CONV_DIM, D_INNER + CONV_DIM + LANES)
SSD_IN_PAD = D_INNER + CONV_DIM + LANES


def _rmsnorm(x, g):
    ms = jnp.mean(x * x, axis=-1, keepdims=True)
    return x * lax.rsqrt(ms + RMS_EPS) * g


def _gelu(x):
    return 0.5 * x * (1.0 + lax.erf(x * SQRT_HALF))


def _silu(x):
    h = 0.5 * x
    return h + h * jnp.tanh(h)


def _dot(a, b):
    return jnp.dot(a, b, preferred_element_type=F32)


def _resident(shape):
    nd = len(shape)
    return pl.BlockSpec(shape, lambda *_: (0,) * nd, pipeline_mode=pl.Buffered(1))


def _layer_weight(shape, layer):
    nd = len(shape)
    return pl.BlockSpec((None,) + tuple(shape), lambda *_: (layer,) + (0,) * nd,
                        pipeline_mode=pl.Buffered(1))


def _params(n_axes):
    return pltpu.CompilerParams(
        dimension_semantics=("arbitrary",) * n_axes, vmem_limit_bytes=VMEM_LIMIT)


def _mlp_kernel(x_ref, g_ref, wup_ref, wdown_ref, gf_ref, o_ref, *, final):
    x = x_ref[...]
    hn = _rmsnorm(x, g_ref[...]).astype(BF16)
    acc = x
    for c in range(D_FF // FF_BLOCK):
        cols = slice(c * FF_BLOCK, (c + 1) * FF_BLOCK)
        a = jnp.maximum(_dot(hn, wup_ref[:, cols]), 0.0)
        acc = acc + _dot((a * a).astype(BF16), wdown_ref[cols, :])
    if final:
        acc = _rmsnorm(acc, gf_ref[...])
    o_ref[...] = acc


def _mlp(x, layer, g, w_up, w_down, g_final, final):
    n = x.shape[0]
    rows = min(MLP_ROWS, n)
    return pl.pallas_call(
        functools.partial(_mlp_kernel, final=final),
        out_shape=jax.ShapeDtypeStruct((n, D_MODEL), F32),
        grid=(n // rows,),
        in_specs=[pl.BlockSpec((rows, D_MODEL), lambda i: (i, 0)),
                  _resident((1, D_MODEL)),
                  _layer_weight((D_MODEL, D_FF), layer),
                  _layer_weight((D_FF, D_MODEL), layer),
                  _resident((1, D_MODEL))],
        out_specs=pl.BlockSpec((rows, D_MODEL), lambda i: (i, 0)),
        compiler_params=_params(1),
        name="mlp",
    )(x, g, w_up, w_down, g_final)


def _cm_prompt_kernel(x_ref, g_ref, win_ref, bin_ref, vg_ref, ws_ref, bs_ref, wout_ref,
                      o_ref, u_sc, v_sc, y_sc):
    rows = x_ref.shape[0]
    x = x_ref[...]
    hn = _rmsnorm(x, g_ref[...]).astype(BF16)
    for c in range(CM_WIDTH // COL_BLOCK):
        cols = slice(c * COL_BLOCK, (c + 1) * COL_BLOCK)
        u_sc[:, cols] = _gelu(_dot(hn, win_ref[:, cols]) + bin_ref[:, cols])
    ss = jnp.zeros((rows, 1), F32)
    for c in range(CM_WIDTH // COL_BLOCK):
        cols = slice(c * COL_BLOCK, (c + 1) * COL_BLOCK)
        wcols = slice(CM_WIDTH + c * COL_BLOCK, CM_WIDTH + (c + 1) * COL_BLOCK)
        gv = _gelu(_dot(hn, win_ref[:, wcols]) + bin_ref[:, wcols])
        ss = ss + jnp.sum(gv * gv, axis=-1, keepdims=True)
        v_sc[:, cols] = gv
    r = lax.rsqrt(ss * (1.0 / CM_WIDTH) + RMS_EPS)
    t_idx = lax.broadcasted_iota(jnp.int32, (CM_CHUNK, CM_CHUNK), 0)
    s_idx = lax.broadcasted_iota(jnp.int32, (CM_CHUNK, CM_CHUNK), 1)
    causal = t_idx >= s_idx
    for g in range(CM_GROUPS):
        cols = slice(g * CM_GROUP_DIM, (g + 1) * CM_GROUP_DIM)
        w = jnp.where(causal, ws_ref[g], 0.0).astype(BF16)
        for c in range(rows // CM_CHUNK):
            rs = slice(c * CM_CHUNK, (c + 1) * CM_CHUNK)
            v = (v_sc[rs, cols] * r[rs] * vg_ref[:, cols]).astype(BF16)
            s = _dot(w, v) + bs_ref[:, cols]
            y_sc[rs, cols] = (u_sc[rs, cols] * s).astype(BF16)
    o_ref[...] = x + _dot(y_sc[...], wout_ref[...])


def _cm_prompt(x, layer, g, w_in, b_in, v_g, w_s, b_s_wide, w_out):
    n = x.shape[0]
    rows = min(CM_ROWS, n)
    return pl.pallas_call(
        _cm_prompt_kernel,
        out_shape=jax.ShapeDtypeStruct((n, D_MODEL), F32),
        grid=(n // rows,),
        in_specs=[pl.BlockSpec((rows, D_MODEL), lambda i: (i, 0)),
                  _resident((1, D_MODEL)),
                  _layer_weight((D_MODEL, 2 * CM_WIDTH), layer),
                  _resident((1, 2 * CM_WIDTH)),
                  _resident((1, CM_WIDTH)),
                  _layer_weight((CM_GROUPS, CM_CHUNK, CM_CHUNK), layer),
                  _resident((CM_CHUNK, CM_WIDTH)),
                  _layer_weight((CM_WIDTH, D_MODEL), layer)],
        out_specs=pl.BlockSpec((rows, D_MODEL), lambda i: (i, 0)),
        scratch_shapes=[pltpu.VMEM((rows, CM_WIDTH), F32),
                        pltpu.VMEM((rows, CM_WIDTH), F32),
                        pltpu.VMEM((rows, CM_WIDTH), BF16)],
        compiler_params=_params(1),
        name="cm_prompt",
    )(x, g, w_in, b_in, v_g, w_s, b_s_wide, w_out)


def _cm_sample_kernel(x_ref, g_ref, win_ref, bin_ref, vg_ref, wdiag_ref, bdiag_ref, wout_ref,
                      o_ref, v_ref):
    x = x_ref[...]
    hn = _rmsnorm(x, g_ref[...]).astype(BF16)
    u = _gelu(_dot(hn, win_ref[:, :CM_WIDTH]) + bin_ref[:, :CM_WIDTH])
    gv = _gelu(_dot(hn, win_ref[:, CM_WIDTH:]) + bin_ref[:, CM_WIDTH:])
    v = _rmsnorm(gv, vg_ref[...])
    v_ref[...] = v
    s = v * wdiag_ref[...] + bdiag_ref[...]
    o_ref[...] = x + _dot((u * s).astype(BF16), wout_ref[...])


def _cm_sample(x, layer, g, w_in, b_in, v_g, w_diag, b_diag, w_out):
    n = x.shape[0]
    return pl.pallas_call(
        _cm_sample_kernel,
        out_shape=(jax.ShapeDtypeStruct((n, D_MODEL), F32),
                   jax.ShapeDtypeStruct((n, CM_WIDTH), F32)),
        grid=(1,),
        in_specs=[_resident((n, D_MODEL)),
                  _resident((1, D_MODEL)),
                  _layer_weight((D_MODEL, 2 * CM_WIDTH), layer),
                  _resident((1, 2 * CM_WIDTH)),
                  _resident((1, CM_WIDTH)),
                  _resident((1, CM_WIDTH)),
                  _resident((1, CM_WIDTH)),
                  _layer_weight((CM_WIDTH, D_MODEL), layer)],
        out_specs=(pl.BlockSpec((n, D_MODEL), lambda i: (0, 0)),
                   pl.BlockSpec((n, CM_WIDTH), lambda i: (0, 0))),
        compiler_params=_params(1),
        name="cm_sample",
    )(x, g, w_in, b_in, v_g, w_diag, b_diag, w_out)


def _ssd_prompt_kernel(x_ref, g_ref, win_ref, convw_ref, convb_ref, dtb_ref,
                       alog_ref, dskip_ref, ng_ref, wout_ref,
                       o_ref, ssm_ref, convo_ref,
                       ext_sc, carry_sc, z_sc, dt_sc, act_sc, yn_sc, ht_sc):
    rows = x_ref.shape[0]
    n_chunks = rows // CHUNK
    step = pl.program_id(1)

    @pl.when(step == 0)
    def _():
        carry_sc[...] = jnp.zeros(carry_sc.shape, F32)
        ht_sc[...] = jnp.zeros(ht_sc.shape, F32)

    def token_of(r):
        return (r % SUBLANES) * CHUNK_TILES + r // SUBLANES

    q_idx = lax.broadcasted_iota(jnp.int32, (CHUNK, CHUNK), 0)
    s_idx = lax.broadcasted_iota(jnp.int32, (CHUNK, CHUNK), 1)
    to_rows = (s_idx == token_of(q_idx)).astype(BF16)
    to_tokens = (q_idx == token_of(s_idx)).astype(BF16)
    causal = token_of(q_idx) >= token_of(s_idx)
    tril = causal.astype(F32)
    low = s_idx < HEAD_DIM

    hn = []
    for c in range(n_chunks):
        h = _rmsnorm(x_ref[c * CHUNK:(c + 1) * CHUNK, :], g_ref[...]).astype(BF16)
        hn.append(_dot(to_rows, h).astype(BF16))
    hn = jnp.concatenate(hn, axis=0)
    z_sc[...] = _dot(hn, win_ref[:, Z_COLS])
    dt_sc[...] = _dot(hn, win_ref[:, DT_COLS])
    xbc = _dot(hn, win_ref[:, XBC_COLS])
    for c in range(n_chunks):
        ext_sc[c * CHUNK_EXT + WRAP_ROWS:(c + 1) * CHUNK_EXT, :] = xbc[c * CHUNK:(c + 1) * CHUNK]

    sub0 = lax.broadcasted_iota(jnp.int32, (WRAP_ROWS, COL_BLOCK), 0) % SUBLANES == 0
    for c in range(n_chunks):
        base = c * CHUNK_EXT
        for cblk in range(CONV_DIM // COL_BLOCK):
            cols = slice(cblk * COL_BLOCK, (cblk + 1) * COL_BLOCK)
            moved = jnp.concatenate(
                [pltpu.roll(ext_sc[base + CHUNK_EXT - WRAP_ROWS + m * SUBLANES:
                                   base + CHUNK_EXT - WRAP_ROWS + (m + 1) * SUBLANES, cols], 1, 0)
                 for m in range(CONV_K - 1)], axis=0)
            ext_sc[base:base + WRAP_ROWS, cols] = jnp.where(sub0, carry_sc[:, cols], moved)
            carry_sc[:, cols] = moved
            conv = convb_ref[:, cols] + ext_sc[base:base + CHUNK, cols] * convw_ref[0:1, cols]
            for k in range(1, CONV_K):
                conv = conv + (ext_sc[base + k * SUBLANES:base + k * SUBLANES + CHUNK, cols]
                               * convw_ref[k:k + 1, cols])
            act_sc[c * CHUNK:(c + 1) * CHUNK, cols] = _silu(conv)
    last = n_chunks * CHUNK_EXT
    for m in range(CONV_K - 1):
        r = last - (CONV_K - 1 - m) * SUBLANES + SUBLANES - 1
        convo_ref[0, m:m + 1, :] = ext_sc[r:r + 1, :]

    a_row = -jnp.exp(alog_ref[...])
    b_off = D_INNER
    c_off = D_INNER + GROUPS * STATE

    for c in range(n_chunks):
        rs = slice(c * CHUNK, (c + 1) * CHUNK)
        dt = jax.nn.softplus(dt_sc[rs, :] + dtb_ref[...])
        a = dt * a_row
        acum = jnp.dot(tril, a, precision=lax.Precision.HIGHEST, preferred_element_type=F32)
        acum_t = acum.T
        dt_t = dt.T
        w_t = jnp.exp(acum_t[:, CHUNK - 1:CHUNK] - acum_t) * dt_t
        cdec = jnp.exp(acum[CHUNK - 1:CHUNK, :])
        arow_t = acum_t - jnp.log(dt_t)
        for g in range(GROUPS):
            bg = act_sc[rs, b_off + g * STATE:b_off + (g + 1) * STATE]
            cg = act_sc[rs, c_off + g * STATE:c_off + (g + 1) * STATE].astype(BF16)
            bt = bg.T
            cb = _dot(cg, bt.astype(BF16))
            ht = ht_sc[g]
            y_off = _dot(cg, ht.astype(BF16))
            ys = []
            for pr in range(HEADS_PER_GROUP // 2):
                h0 = g * HEADS_PER_GROUP + 2 * pr
                lanes = slice(h0 * HEAD_DIM, (h0 + 2) * HEAD_DIM)
                half = slice(pr * LANES, (pr + 1) * LANES)
                xs = act_sc[rs, lanes]
                rhs = jnp.concatenate([jnp.where(low, xs, 0.0).astype(BF16),
                                       jnp.where(low, 0.0, xs).astype(BF16)], axis=0)
                ms, bs, cols_b = [], [], []
                for hd in (h0, h0 + 1):
                    col_b = jnp.broadcast_to(acum[:, hd:hd + 1], (CHUNK, CHUNK))
                    seg = col_b - arow_t[hd:hd + 1, :]
                    ms.append((cb * jnp.exp(jnp.where(causal, seg, -jnp.inf))).astype(BF16))
                    bs.append((bt * w_t[hd:hd + 1, :]).astype(BF16))
                    cols_b.append(col_b)
                lhs = jnp.concatenate([jnp.concatenate(ms, axis=1),
                                       jnp.concatenate(bs, axis=1)], axis=0)
                res = _dot(lhs, rhs)
                e = jnp.exp(jnp.where(low, cols_b[0], cols_b[1]))
                y = res[0:CHUNK] + e * y_off[:, half] + xs * dskip_ref[:, lanes]
                cd = jnp.where(low[0:1], cdec[:, h0:h0 + 1], cdec[:, h0 + 1:h0 + 2])
                ht_sc[g, :, half] = ht[:, half] * cd + res[CHUNK:2 * CHUNK]
                ys.append(y)
            gcols = slice(g * GROUP_WIDTH, (g + 1) * GROUP_WIDTH)
            yg = jnp.concatenate(ys, axis=1)
            zg = z_sc[rs, gcols]
            yg = yg * _silu(zg)
            yg = yg * lax.rsqrt(jnp.mean(yg * yg, axis=-1, keepdims=True) + RMS_EPS)
            yn_sc[rs, gcols] = (yg * ng_ref[:, gcols]).astype(BF16)
        yn = _dot(to_tokens, yn_sc[rs, :]).astype(BF16)
        o_ref[rs, :] = x_ref[rs, :] + _dot(yn, wout_ref[...])

    @pl.when(step == pl.num_programs(1) - 1)
    def _():
        for g in range(GROUPS):
            ssm_ref[0, g * GROUP_WIDTH:(g + 1) * GROUP_WIDTH, :] = ht_sc[g].T


def _ssd_prompt(x, batch, layer, g, w_in, conv_w, conv_b, dt_bias, a_log, d_wide, norm_g, w_out):
    n = x.shape[0]
    seq = n // batch
    rows = min(SSD_ROWS, seq)
    steps = seq // rows
    return pl.pallas_call(
        _ssd_prompt_kernel,
        out_shape=(jax.ShapeDtypeStruct((n, D_MODEL), F32),
                   jax.ShapeDtypeStruct((batch, D_INNER, STATE), F32),
                   jax.ShapeDtypeStruct((batch, CONV_K - 1, CONV_DIM), F32)),
        grid=(batch, steps),
        in_specs=[pl.BlockSpec((rows, D_MODEL), lambda b, s: (b * steps + s, 0)),
                  _resident((1, D_MODEL)),
                  _layer_weight((D_MODEL, SSD_IN_PAD), layer),
                  _resident((CONV_K, CONV_DIM)),
                  _resident((1, CONV_DIM)),
                  _resident((1, LANES)),
                  _resident((1, LANES)),
                  _resident((1, D_INNER)),
                  _resident((1, D_INNER)),
                  _layer_weight((D_INNER, D_MODEL), layer)],
        out_specs=(pl.BlockSpec((rows, D_MODEL), lambda b, s: (b * steps + s, 0)),
                   pl.BlockSpec((1, D_INNER, STATE), lambda b, s: (b, 0, 0)),
                   pl.BlockSpec((1, CONV_K - 1, CONV_DIM), lambda b, s: (b, 0, 0))),
        scratch_shapes=[pltpu.VMEM((rows // CHUNK * CHUNK_EXT, CONV_DIM), F32),
                        pltpu.VMEM((WRAP_ROWS, CONV_DIM), F32),
                        pltpu.VMEM((rows, D_INNER), F32),
                        pltpu.VMEM((rows, LANES), F32),
                        pltpu.VMEM((rows, CONV_DIM), F32),
                        pltpu.VMEM((rows, D_INNER), BF16),
                        pltpu.VMEM((GROUPS, STATE, GROUP_WIDTH), F32)],
        compiler_params=_params(2),
        name="ssd_prompt",
    )(x, g, w_in, conv_w, conv_b, dt_bias, a_log, d_wide, norm_g, w_out)


def _ssd_sample_in_kernel(x_ref, g_ref, win_ref, cs_ref, convw_ref, convb_ref,
                          dtb_ref, alog_ref, expand_ref,
                          z_ref, xs_ref, b_ref, c_ref, xdt_ref, dec_ref, convo_ref):
    hn = _rmsnorm(x_ref[...], g_ref[...]).astype(BF16)
    z_ref[...] = _dot(hn, win_ref[:, Z_COLS])
    xbc = _dot(hn, win_ref[:, XBC_COLS])
    dt_raw = _dot(hn, win_ref[:, DT_COLS])
    conv = convb_ref[...] + cs_ref[:, 0:CONV_DIM] * convw_ref[0:1, :]
    for k in range(1, CONV_K - 1):
        conv = conv + cs_ref[:, k * CONV_DIM:(k + 1) * CONV_DIM] * convw_ref[k:k + 1, :]
    conv = conv + xbc * convw_ref[CONV_K - 1:CONV_K, :]
    convo_ref[:, 0:(CONV_K - 2) * CONV_DIM] = cs_ref[:, CONV_DIM:(CONV_K - 1) * CONV_DIM]
    convo_ref[:, (CONV_K - 2) * CONV_DIM:] = xbc
    act = _silu(conv)
    xs = act[:, :D_INNER]
    xs_ref[...] = xs
    b_ref[...] = act[:, D_INNER:D_INNER + GROUPS * STATE]
    c_ref[...] = act[:, D_INNER + GROUPS * STATE:]
    dt = jax.nn.softplus(dt_raw + dtb_ref[...])
    dec = jnp.exp(dt * (-jnp.exp(alog_ref[...])))

    def widen(v):
        hi = v.astype(BF16)
        lo = (v - hi.astype(F32)).astype(BF16)
        return _dot(hi, expand_ref[...]) + _dot(lo, expand_ref[...])

    xdt_ref[...] = xs * widen(dt)
    dec_ref[...] = dec


def _ssd_sample_state_kernel(dec_ref, h_ref, xdt_ref, b_ref, c_ref, stacked_ref, ho_ref, y_ref):
    del stacked_ref
    phase = pl.program_id(0)

    @pl.when(phase != 0)
    def _():
        ho_ref[...] = jnp.zeros(ho_ref.shape, F32)

    @pl.when(phase == 0)
    def _():
        base = pl.program_id(1) * SAMPLE_BLOCK
        g_idx = lax.broadcasted_iota(jnp.int32, (GROUPS, D_INNER), 0)
        l_idx = lax.broadcasted_iota(jnp.int32, (GROUPS, D_INNER), 1)
        own = (l_idx // GROUP_WIDTH) == g_idx
        for bi in range(SAMPLE_BLOCK):
            u_t = jnp.where(own, xdt_ref[bi], 0.0).astype(BF16)
            new = lax.dot_general(u_t, b_ref[bi].astype(BF16), (((0,), (0,)), ((), ())),
                                  preferred_element_type=F32)
            parts = []
            for hd in range(HEADS):
                rows = slice(hd * HEAD_DIM, (hd + 1) * HEAD_DIM)
                parts.append(h_ref[bi, rows, :] * dec_ref[base + bi, hd] + new[rows])
            h_new = jnp.concatenate(parts, axis=0)
            ho_ref[bi] = h_new
            y_all = lax.dot_general(c_ref[bi].astype(BF16), h_new.astype(BF16),
                                    (((1,), (1,)), ((), ())), preferred_element_type=F32)
            y_ref[bi] = jnp.sum(jnp.where(own, y_all, 0.0), axis=0, keepdims=True)


def _ssd_sample_out_kernel(x_ref, y_ref, xs_ref, z_ref, dskip_ref, ng_ref, wout_ref, o_ref):
    y = y_ref[...] + xs_ref[...] * dskip_ref[...]
    z = z_ref[...]
    y = y * _silu(z)
    parts = []
    for g in range(GROUPS):
        yg = y[:, g * GROUP_WIDTH:(g + 1) * GROUP_WIDTH]
        parts.append(yg * lax.rsqrt(jnp.mean(yg * yg, axis=-1, keepdims=True) + RMS_EPS))
    yn = (jnp.concatenate(parts, axis=1) * ng_ref[...]).astype(BF16)
    o_ref[...] = x_ref[...] + _dot(yn, wout_ref[...])


def _ssd_sample(x, ssm_all, conv_all, ssm_new_all, layer, g, w_in, conv_w, conv_b,
                dt_bias, a_log, expand, d_wide, norm_g, w_out):
    n = x.shape[0]
    n_layers = ssm_all.shape[0]
    gs = GROUPS * STATE
    full = lambda shape: pl.BlockSpec(shape, lambda i: (0,) * len(shape))
    z, xs, bm, cm, xdt, dec, conv_new = pl.pallas_call(
        _ssd_sample_in_kernel,
        out_shape=(jax.ShapeDtypeStruct((n, D_INNER), F32),
                   jax.ShapeDtypeStruct((n, D_INNER), F32),
                   jax.ShapeDtypeStruct((n, gs), F32),
                   jax.ShapeDtypeStruct((n, gs), F32),
                   jax.ShapeDtypeStruct((n, D_INNER), F32),
                   jax.ShapeDtypeStruct((n, LANES), F32),
                   jax.ShapeDtypeStruct((n, (CONV_K - 1) * CONV_DIM), F32)),
        grid=(1,),
        in_specs=[_resident((n, D_MODEL)),
                  _resident((1, D_MODEL)),
                  _layer_weight((D_MODEL, SSD_IN_PAD), layer),
                  pl.BlockSpec((n, (CONV_K - 1) * CONV_DIM), lambda i: (layer, 0),
                               pipeline_mode=pl.Buffered(1)),
                  _resident((CONV_K, CONV_DIM)),
                  _resident((1, CONV_DIM)),
                  _resident((1, LANES)),
                  _resident((1, LANES)),
                  _resident((LANES, D_INNER))],
        out_specs=(full((n, D_INNER)), full((n, D_INNER)), full((n, gs)), full((n, gs)),
                   full((n, D_INNER)), full((n, LANES)), full((n, (CONV_K - 1) * CONV_DIM))),
        compiler_params=_params(1),
        name="ssd_sample_in",
    )(x, g, w_in, conv_all.reshape(n_layers * n, (CONV_K - 1) * CONV_DIM), conv_w, conv_b,
      dt_bias, a_log, expand)

    blk = SAMPLE_BLOCK
    steps = n // blk
    first = ssm_new_all is None
    phases = n_layers - layer if first else 1
    held = lambda p, i: (jnp.where(p == 0, i, steps - 1), 0, 0)
    row = pl.BlockSpec((blk, 1, D_INNER), held)
    grp = pl.BlockSpec((blk, GROUPS, STATE), held)
    h_old = pl.BlockSpec((blk, D_INNER, STATE),
                         lambda p, i: (layer * steps + jnp.where(p == 0, i, steps - 1), 0, 0))
    h_new = pl.BlockSpec((blk, D_INNER, STATE), lambda p, i: ((layer + p) * steps + i, 0, 0))
    carried = ssm_all.reshape(n_layers * n, D_INNER, STATE)
    ssm_new_all, y = pl.pallas_call(
        _ssd_sample_state_kernel,
        out_shape=(jax.ShapeDtypeStruct((n_layers * n, D_INNER, STATE), F32),
                   jax.ShapeDtypeStruct((n, 1, D_INNER), F32)),
        grid=(phases, steps),
        in_specs=[pl.BlockSpec(memory_space=pltpu.SMEM), h_old, row, grp, grp,
                  pl.BlockSpec(memory_space=pl.ANY)],
        out_specs=(h_new, row),
        input_output_aliases={} if first else {5: 0},
        compiler_params=_params(2),
        name="ssd_sample_state",
    )(dec[:, :HEADS], carried, xdt.reshape(n, 1, D_INNER), bm.reshape(n, GROUPS, STATE),
      cm.reshape(n, GROUPS, STATE), carried if first else ssm_new_all)

    x_new = pl.pallas_call(
        _ssd_sample_out_kernel,
        out_shape=jax.ShapeDtypeStruct((n, D_MODEL), F32),
        grid=(1,),
        in_specs=[_resident((n, D_MODEL)), _resident((n, D_INNER)), _resident((n, D_INNER)),
                  _resident((n, D_INNER)), _resident((1, D_INNER)), _resident((1, D_INNER)),
                  _layer_weight((D_INNER, D_MODEL), layer)],
        out_specs=full((n, D_MODEL)),
        compiler_params=_params(1),
        name="ssd_sample_out",
    )(x, y.reshape(n, D_INNER), xs, z, d_wide, norm_g, w_out)
    return x_new, ssm_new_all, conv_new.reshape(n, CONV_K - 1, CONV_DIM)


def _pad_lanes(v):
    return jnp.pad(v.reshape(1, -1), ((0, 0), (0, LANES - v.shape[-1])))


def kernel(x_prompt, x_sample, state_ssm, state_conv, norm_mix_g, norm_mlp_g, norm_final_g,
           ssd_w_in, ssd_conv_w, ssd_conv_b, ssd_dt_bias, ssd_a_log, ssd_d, ssd_norm_g,
           ssd_w_out, cm_w_in, cm_b_in, cm_v_norm_g, cm_w_s, cm_b_s, cm_w_out,
           mlp_w_up, mlp_w_down):
    batch, seq, _ = x_prompt.shape
    dec_batch, dec_seq, _ = x_sample.shape
    assert dec_seq == 1 and seq % CHUNK == 0
    pos = PAST_LEN % CM_CHUNK

    xp = x_prompt.reshape(batch * seq, D_MODEL)
    xs = x_sample.reshape(dec_batch, D_MODEL)
    row = lambda v: v.reshape(1, -1)
    expand = jnp.asarray(np.kron(np.eye(LANES, HEADS, dtype=np.float32),
                                 np.ones((1, HEAD_DIM), np.float32)), BF16)

    w_ssd_in = jnp.concatenate(
        [ssd_w_in.astype(BF16), jnp.zeros(ssd_w_in.shape[:2] + (LANES - HEADS,), BF16)], axis=2)
    w_ssd_out = ssd_w_out.astype(BF16)
    w_cm_in = cm_w_in.astype(BF16)
    w_cm_out = cm_w_out.astype(BF16)
    w_up = mlp_w_up.astype(BF16)
    w_down = mlp_w_down.astype(BF16)

    ssm_p, conv_p, conv_s, v_s = [], [], [], []
    ssm_s = None
    for i in range(DEPTH):
        j = i // 2
        g_mix = row(norm_mix_g[i])
        if i % 2 == 0:
            shared = (ssd_conv_w[j], row(ssd_conv_b[j]), _pad_lanes(ssd_dt_bias[j]),
                      _pad_lanes(ssd_a_log[j]))
            d_wide = jnp.repeat(ssd_d[j], HEAD_DIM).reshape(1, D_INNER)
            n_g = row(ssd_norm_g[j])
            xp, s_p, c_p = _ssd_prompt(xp, batch, j, g_mix, w_ssd_in, *shared, d_wide, n_g, w_ssd_out)
            ssm_p.append(s_p.reshape(batch, HEADS, HEAD_DIM, STATE))
            conv_p.append(c_p)
            xs, ssm_s, c_s = _ssd_sample(xs, state_ssm, state_conv, ssm_s, j, g_mix, w_ssd_in,
                                         *shared, expand, d_wide, n_g, w_ssd_out)
            conv_s.append(c_s)
        else:
            b_in = row(cm_b_in[j])
            v_g = row(cm_v_norm_g[j])
            b_s_wide = jnp.repeat(cm_b_s[j].T, CM_GROUP_DIM, axis=1)
            xp = _cm_prompt(xp, j, g_mix, w_cm_in, b_in, v_g, cm_w_s, b_s_wide, w_cm_out)
            w_diag = jnp.repeat(cm_w_s[j][:, pos, pos], CM_GROUP_DIM).reshape(1, CM_WIDTH)
            b_diag = jnp.repeat(cm_b_s[j][:, pos], CM_GROUP_DIM).reshape(1, CM_WIDTH)
            xs, v = _cm_sample(xs, j, g_mix, w_cm_in, b_in, v_g, w_diag, b_diag, w_cm_out)
            v_s.append(v.reshape(dec_batch, dec_seq, CM_WIDTH))
        final = i == DEPTH - 1
        g_mlp = row(norm_mlp_g[i])
        g_fin = row(norm_final_g)
        xp = _mlp(xp, i, g_mlp, w_up, w_down, g_fin, final)
        xs = _mlp(xs, i, g_mlp, w_up, w_down, g_fin, final)

    return (xp.reshape(batch, seq, D_MODEL), xs.reshape(dec_batch, dec_seq, D_MODEL),
            jnp.stack(ssm_p), jnp.stack(conv_p),
            ssm_s.reshape(state_ssm.shape), jnp.stack(conv_s), jnp.stack(v_s))
```

```python
import functools

import numpy as np
import jax
import jax.numpy as jnp
from jax import lax
from jax.experimental import pallas as pl
from jax.experimental.pallas import tpu as pltpu

F32 = jnp.float32
BF16 = jnp.bfloat16

D_MODEL = 1024
DEPTH = 4
D_INNER = 2048
HEAD_DIM = 64
HEADS = 32
GROUPS = 8
HEADS_PER_GROUP = 4
STATE = 128
CONV_K = 4
CHUNK = 128
CONV_DIM = D_INNER + 2 * GROUPS * STATE
GROUP_WIDTH = D_INNER // GROUPS
CM_CHUNK = 128
CM_WIDTH = 2 * D_MODEL
CM_GROUPS = 8
CM_GROUP_DIM = CM_WIDTH // CM_GROUPS
D_FF = 4 * D_MODEL
PAST_LEN = 16384
RMS_EPS = 1e-5

LANES = 128
SUBLANES = 8
CHUNK_TILES = CHUNK // SUBLANES
WRAP_ROWS = (CONV_K - 1) * SUBLANES
CHUNK_EXT = WRAP_ROWS + CHUNK
VMEM_LIMIT = 56 * 1024 * 1024
SSD_ROWS = 512
CM_ROWS = 512
MLP_ROWS = 512
FF_BLOCK = 1024
COL_BLOCK = 512
SAMPLE_BLOCK = 4
SQRT_HALF = np.sqrt(0.5).astype(np.float32)
Z_COLS = slice(0, D_INNER)
XBC_COLS = slice(D_INNER, D_INNER + CONV_DIM)
SSD_IN_MAIN = D_INNER + CONV_DIM


def _rmsnorm(x, g):
    ms = jnp.mean(x * x, axis=-1, keepdims=True)
    return x * lax.rsqrt(ms + RMS_EPS) * g


def _gelu(x):
    return 0.5 * x * (1.0 + lax.erf(x * SQRT_HALF))


def _silu(x):
    h = 0.5 * x
    return h + h * jnp.tanh(h)


def _dot(a, b):
    return jnp.dot(a, b, preferred_element_type=F32)


def _resident(shape):
    nd = len(shape)
    return pl.BlockSpec(shape, lambda *_: (0,) * nd, pipeline_mode=pl.Buffered(1))


def _layer_weight(shape, layer):
    nd = len(shape)
    return pl.BlockSpec((None,) + tuple(shape), lambda *_: (layer,) + (0,) * nd,
                        pipeline_mode=pl.Buffered(1))


def _params(n_axes):
    return pltpu.CompilerParams(
        dimension_semantics=("arbitrary",) * n_axes, vmem_limit_bytes=VMEM_LIMIT)


def _mlp_kernel(xp_ref, xs_ref, g_ref, wup_ref, wdown_ref, gf_ref, op_ref, os_ref, *, final):
    def rows_through_mlp(x_ref, o_ref):
        x = x_ref[...]
        hn = _rmsnorm(x, g_ref[...]).astype(BF16)
        acc = x
        for c in range(D_FF // FF_BLOCK):
            cols = slice(c * FF_BLOCK, (c + 1) * FF_BLOCK)
            a = jnp.maximum(_dot(hn, wup_ref[:, cols]), 0.0)
            acc = acc + _dot((a * a).astype(BF16), wdown_ref[cols, :])
        if final:
            acc = _rmsnorm(acc, gf_ref[...])
        o_ref[...] = acc

    last = pl.num_programs(0) - 1
    pl.when(pl.program_id(0) < last)(functools.partial(rows_through_mlp, xp_ref, op_ref))
    pl.when(pl.program_id(0) == last)(functools.partial(rows_through_mlp, xs_ref, os_ref))


def _mlp(xp, xs, layer, g, w_up, w_down, g_final, final):
    n, ns = xp.shape[0], xs.shape[0]
    rows = min(MLP_ROWS, n)
    tiles = n // rows
    tile = pl.BlockSpec((rows, D_MODEL), lambda i: (jnp.minimum(i, tiles - 1), 0))
    return pl.pallas_call(
        functools.partial(_mlp_kernel, final=final),
        out_shape=(jax.ShapeDtypeStruct((n, D_MODEL), F32),
                   jax.ShapeDtypeStruct((ns, D_MODEL), F32)),
        grid=(tiles + 1,),
        in_specs=[tile,
                  _resident((ns, D_MODEL)),
                  _resident((1, D_MODEL)),
                  _layer_weight((D_MODEL, D_FF), layer),
                  _layer_weight((D_FF, D_MODEL), layer),
                  _resident((1, D_MODEL))],
        out_specs=(tile, pl.BlockSpec((ns, D_MODEL), lambda i: (0, 0))),
        compiler_params=_params(1),
        name="mlp",
    )(xp, xs, g, w_up, w_down, g_final)


def _cm_kernel(xp_ref, xs_ref, g_ref, win_ref, bin_ref, vg_ref, ws_ref, bs_ref, wdiag_ref, bdiag_ref,
               wout_ref, op_ref, os_ref, vs_ref, u_sc, v_sc, y_sc):
    last = pl.num_programs(0) - 1
    pl.when(pl.program_id(0) < last)(functools.partial(
        _cm_prompt_tile, xp_ref, g_ref, win_ref, bin_ref, vg_ref, ws_ref, bs_ref, wout_ref, op_ref,
        u_sc, v_sc, y_sc))
    pl.when(pl.program_id(0) == last)(functools.partial(
        _cm_sample_rows, xs_ref, g_ref, win_ref, bin_ref, vg_ref, wdiag_ref, bdiag_ref, wout_ref,
        os_ref, vs_ref))


def _cm_prompt_tile(x_ref, g_ref, win_ref, bin_ref, vg_ref, ws_ref, bs_ref, wout_ref,
                    o_ref, u_sc, v_sc, y_sc):
    rows = x_ref.shape[0]
    x = x_ref[...]
    hn = _rmsnorm(x, g_ref[...]).astype(BF16)
    for c in range(CM_WIDTH // COL_BLOCK):
        cols = slice(c * COL_BLOCK, (c + 1) * COL_BLOCK)
        u_sc[:, cols] = _gelu(_dot(hn, win_ref[:, cols]) + bin_ref[:, cols])
    ss = jnp.zeros((rows, 1), F32)
    for c in range(CM_WIDTH // COL_BLOCK):
        cols = slice(c * COL_BLOCK, (c + 1) * COL_BLOCK)
        wcols = slice(CM_WIDTH + c * COL_BLOCK, CM_WIDTH + (c + 1) * COL_BLOCK)
        gv = _gelu(_dot(hn, win_ref[:, wcols]) + bin_ref[:, wcols])
        ss = ss + jnp.sum(gv * gv, axis=-1, keepdims=True)
        v_sc[:, cols] = gv
    r = lax.rsqrt(ss * (1.0 / CM_WIDTH) + RMS_EPS)
    t_idx = lax.broadcasted_iota(jnp.int32, (CM_CHUNK, CM_CHUNK), 0)
    s_idx = lax.broadcasted_iota(jnp.int32, (CM_CHUNK, CM_CHUNK), 1)
    causal = t_idx >= s_idx
    for g in range(CM_GROUPS):
        cols = slice(g * CM_GROUP_DIM, (g + 1) * CM_GROUP_DIM)
        w = jnp.where(causal, ws_ref[g], 0.0).astype(BF16)
        for c in range(rows // CM_CHUNK):
            rs = slice(c * CM_CHUNK, (c + 1) * CM_CHUNK)
            v = (v_sc[rs, cols] * r[rs] * vg_ref[:, cols]).astype(BF16)
            s = _dot(w, v) + bs_ref[:, cols]
            y_sc[rs, cols] = (u_sc[rs, cols] * s).astype(BF16)
    o_ref[...] = x + _dot(y_sc[...], wout_ref[...])


def _cm_sample_rows(x_ref, g_ref, win_ref, bin_ref, vg_ref, wdiag_ref, bdiag_ref, wout_ref,
                    o_ref, v_ref):
    x = x_ref[...]
    hn = _rmsnorm(x, g_ref[...]).astype(BF16)
    u = _gelu(_dot(hn, win_ref[:, :CM_WIDTH]) + bin_ref[:, :CM_WIDTH])
    gv = _gelu(_dot(hn, win_ref[:, CM_WIDTH:]) + bin_ref[:, CM_WIDTH:])
    v = _rmsnorm(gv, vg_ref[...])
    v_ref[...] = v
    s = v * wdiag_ref[...] + bdiag_ref[...]
    o_ref[...] = x + _dot((u * s).astype(BF16), wout_ref[...])


def _cm(xp, xs, layer, g, w_in, b_in, v_g, w_s, b_s_wide, w_diag, b_diag, w_out):
    n, ns = xp.shape[0], xs.shape[0]
    rows = min(CM_ROWS, n)
    tiles = n // rows
    tile = pl.BlockSpec((rows, D_MODEL), lambda i: (jnp.minimum(i, tiles - 1), 0))
    whole = lambda width: pl.BlockSpec((ns, width), lambda i: (0, 0))
    return pl.pallas_call(
        _cm_kernel,
        out_shape=(jax.ShapeDtypeStruct((n, D_MODEL), F32),
                   jax.ShapeDtypeStruct((ns, D_MODEL), F32),
                   jax.ShapeDtypeStruct((ns, CM_WIDTH), F32)),
        grid=(tiles + 1,),
        in_specs=[tile,
                  _resident((ns, D_MODEL)),
                  _resident((1, D_MODEL)),
                  _layer_weight((D_MODEL, 2 * CM_WIDTH), layer),
                  _resident((1, 2 * CM_WIDTH)),
                  _resident((1, CM_WIDTH)),
                  _layer_weight((CM_GROUPS, CM_CHUNK, CM_CHUNK), layer),
                  _resident((CM_CHUNK, CM_WIDTH)),
                  _resident((1, CM_WIDTH)),
                  _resident((1, CM_WIDTH)),
                  _layer_weight((CM_WIDTH, D_MODEL), layer)],
        out_specs=(tile, whole(D_MODEL), whole(CM_WIDTH)),
        scratch_shapes=[pltpu.VMEM((rows, CM_WIDTH), F32),
                        pltpu.VMEM((rows, CM_WIDTH), F32),
                        pltpu.VMEM((rows, CM_WIDTH), BF16)],
        compiler_params=_params(1),
        name="cm",
    )(xp, xs, g, w_in, b_in, v_g, w_s, b_s_wide, w_diag, b_diag, w_out)


def _ssd_prompt_kernel(x_ref, g_ref, win_ref, wdt_ref, convw_ref, convb_ref, dtb_ref,
                       alog_ref, dskip_ref, ng_ref, wout_ref,
                       o_ref, ssm_ref, convo_ref,
                       ext_sc, carry_sc, z_sc, dt_sc, act_sc, yn_sc, ht_sc):
    rows = x_ref.shape[0]
    n_chunks = rows // CHUNK
    step = pl.program_id(1)

    @pl.when(step == 0)
    def _():
        carry_sc[...] = jnp.zeros(carry_sc.shape, F32)
        ht_sc[...] = jnp.zeros(ht_sc.shape, F32)

    def token_of(r):
        return (r % SUBLANES) * CHUNK_TILES + r // SUBLANES

    q_idx = lax.broadcasted_iota(jnp.int32, (CHUNK, CHUNK), 0)
    s_idx = lax.broadcasted_iota(jnp.int32, (CHUNK, CHUNK), 1)
    to_rows = (s_idx == token_of(q_idx)).astype(BF16)
    to_tokens = (q_idx == token_of(s_idx)).astype(BF16)
    causal = token_of(q_idx) >= token_of(s_idx)
    tril = causal.astype(F32)
    low = s_idx < HEAD_DIM

    hn = []
    for c in range(n_chunks):
        h = _rmsnorm(x_ref[c * CHUNK:(c + 1) * CHUNK, :], g_ref[...]).astype(BF16)
        hn.append(_dot(to_rows, h).astype(BF16))
    hn = jnp.concatenate(hn, axis=0)
    z_sc[...] = _dot(hn, win_ref[:, Z_COLS])
    dt_sc[...] = _dot(hn, wdt_ref[...])
    xbc = _dot(hn, win_ref[:, XBC_COLS])
    for c in range(n_chunks):
        ext_sc[c * CHUNK_EXT + WRAP_ROWS:(c + 1) * CHUNK_EXT, :] = xbc[c * CHUNK:(c + 1) * CHUNK]

    sub0 = lax.broadcasted_iota(jnp.int32, (WRAP_ROWS, COL_BLOCK), 0) % SUBLANES == 0
    for c in range(n_chunks):
        base = c * CHUNK_EXT
        for cblk in range(CONV_DIM // COL_BLOCK):
            cols = slice(cblk * COL_BLOCK, (cblk + 1) * COL_BLOCK)
            moved = jnp.concatenate(
                [pltpu.roll(ext_sc[base + CHUNK_EXT - WRAP_ROWS + m * SUBLANES:
                                   base + CHUNK_EXT - WRAP_ROWS + (m + 1) * SUBLANES, cols], 1, 0)
                 for m in range(CONV_K - 1)], axis=0)
            ext_sc[base:base + WRAP_ROWS, cols] = jnp.where(sub0, carry_sc[:, cols], moved)
            carry_sc[:, cols] = moved
            conv = convb_ref[:, cols] + ext_sc[base:base + CHUNK, cols] * convw_ref[0:1, cols]
            for k in range(1, CONV_K):
                conv = conv + (ext_sc[base + k * SUBLANES:base + k * SUBLANES + CHUNK, cols]
                               * convw_ref[k:k + 1, cols])
            act_sc[c * CHUNK:(c + 1) * CHUNK, cols] = _silu(conv)
    last = n_chunks * CHUNK_EXT
    for m in range(CONV_K - 1):
        r = last - (CONV_K - 1 - m) * SUBLANES + SUBLANES - 1
        convo_ref[0, m:m + 1, :] = ext_sc[r:r + 1, :]

    a_row = -jnp.exp(alog_ref[...])
    b_off = D_INNER
    c_off = D_INNER + GROUPS * STATE

    for c in range(n_chunks):
        rs = slice(c * CHUNK, (c + 1) * CHUNK)
        dt = jax.nn.softplus(dt_sc[rs, :] + dtb_ref[...])
        a = dt * a_row
        acum = jnp.dot(tril, a, precision=lax.Precision.HIGHEST, preferred_element_type=F32)
        acum_t = acum.T
        dt_t = dt.T
        w_t = jnp.exp(acum_t[:, CHUNK - 1:CHUNK] - acum_t) * dt_t
        cdec = jnp.exp(acum[CHUNK - 1:CHUNK, :])
        arow_t = acum_t - jnp.log(dt_t)
        for g in range(GROUPS):
            bg = act_sc[rs, b_off + g * STATE:b_off + (g + 1) * STATE]
            cg = act_sc[rs, c_off + g * STATE:c_off + (g + 1) * STATE].astype(BF16)
            bt = bg.T
            cb = _dot(cg, bt.astype(BF16))
            ht = ht_sc[g]
            y_off = _dot(cg, ht.astype(BF16))
            ys = []
            for pr in range(HEADS_PER_GROUP // 2):
                h0 = g * HEADS_PER_GROUP + 2 * pr
                lanes = slice(h0 * HEAD_DIM, (h0 + 2) * HEAD_DIM)
                half = slice(pr * LANES, (pr + 1) * LANES)
                xs = act_sc[rs, lanes]
                rhs = jnp.concatenate([jnp.where(low, xs, 0.0).astype(BF16),
                                       jnp.where(low, 0.0, xs).astype(BF16)], axis=0)
                ms, bs, cols_b = [], [], []
                for hd in (h0, h0 + 1):
                    col_b = jnp.broadcast_to(acum[:, hd:hd + 1], (CHUNK, CHUNK))
                    seg = col_b - arow_t[hd:hd + 1, :]
                    ms.append((cb * jnp.exp(jnp.where(causal, seg, -jnp.inf))).astype(BF16))
                    bs.append((bt * w_t[hd:hd + 1, :]).astype(BF16))
                    cols_b.append(col_b)
                lhs = jnp.concatenate([jnp.concatenate(ms, axis=1),
                                       jnp.concatenate(bs, axis=1)], axis=0)
                res = _dot(lhs, rhs)
                e = jnp.exp(jnp.where(low, cols_b[0], cols_b[1]))
                y = res[0:CHUNK] + e * y_off[:, half] + xs * dskip_ref[:, lanes]
                cd = jnp.where(low[0:1], cdec[:, h0:h0 + 1], cdec[:, h0 + 1:h0 + 2])
                ht_sc[g, :, half] = ht[:, half] * cd + res[CHUNK:2 * CHUNK]
                ys.append(y)
            gcols = slice(g * GROUP_WIDTH, (g + 1) * GROUP_WIDTH)
            yg = jnp.concatenate(ys, axis=1)
            zg = z_sc[rs, gcols]
            yg = yg * _silu(zg)
            yg = yg * lax.rsqrt(jnp.mean(yg * yg, axis=-1, keepdims=True) + RMS_EPS)
            yn_sc[rs, gcols] = (yg * ng_ref[:, gcols]).astype(BF16)
        yn = _dot(to_tokens, yn_sc[rs, :]).astype(BF16)
        o_ref[rs, :] = x_ref[rs, :] + _dot(yn, wout_ref[...])

    @pl.when(step == pl.num_programs(1) - 1)
    def _():
        for g in range(GROUPS):
            ssm_ref[0, g * GROUP_WIDTH:(g + 1) * GROUP_WIDTH, :] = ht_sc[g].T


def _ssd_prompt(x, batch, layer, g, w_in, w_dt, conv_w, conv_b, dt_bias, a_log, d_wide, norm_g, w_out):
    n = x.shape[0]
    seq = n // batch
    rows = min(SSD_ROWS, seq)
    steps = seq // rows
    return pl.pallas_call(
        _ssd_prompt_kernel,
        out_shape=(jax.ShapeDtypeStruct((n, D_MODEL), F32),
                   jax.ShapeDtypeStruct((batch, D_INNER, STATE), F32),
                   jax.ShapeDtypeStruct((batch, CONV_K - 1, CONV_DIM), F32)),
        grid=(batch, steps),
        in_specs=[pl.BlockSpec((rows, D_MODEL), lambda b, s: (b * steps + s, 0)),
                  _resident((1, D_MODEL)),
                  _layer_weight((D_MODEL, SSD_IN_MAIN), layer),
                  _layer_weight((D_MODEL, LANES), layer),
                  _resident((CONV_K, CONV_DIM)),
                  _resident((1, CONV_DIM)),
                  _resident((1, LANES)),
                  _resident((1, LANES)),
                  _resident((1, D_INNER)),
                  _resident((1, D_INNER)),
                  _layer_weight((D_INNER, D_MODEL), layer)],
        out_specs=(pl.BlockSpec((rows, D_MODEL), lambda b, s: (b * steps + s, 0)),
                   pl.BlockSpec((1, D_INNER, STATE), lambda b, s: (b, 0, 0)),
                   pl.BlockSpec((1, CONV_K - 1, CONV_DIM), lambda b, s: (b, 0, 0))),
        scratch_shapes=[pltpu.VMEM((rows // CHUNK * CHUNK_EXT, CONV_DIM), F32),
                        pltpu.VMEM((WRAP_ROWS, CONV_DIM), F32),
                        pltpu.VMEM((rows, D_INNER), F32),
                        pltpu.VMEM((rows, LANES), F32),
                        pltpu.VMEM((rows, CONV_DIM), F32),
                        pltpu.VMEM((rows, D_INNER), BF16),
                        pltpu.VMEM((GROUPS, STATE, GROUP_WIDTH), F32)],
        compiler_params=_params(2),
        name="ssd_prompt",
    )(x, g, w_in, w_dt, conv_w, conv_b, dt_bias, a_log, d_wide, norm_g, w_out)


def _ssd_sample_in_kernel(x_ref, g_ref, win_ref, wdt_ref, cs_ref, convw_ref, convb_ref,
                          dtb_ref, alog_ref, expand_ref,
                          z_ref, xs_ref, b_ref, c_ref, xdt_ref, dec_ref, convo_ref):
    hn = _rmsnorm(x_ref[...], g_ref[...]).astype(BF16)
    z_ref[...] = _dot(hn, win_ref[:, Z_COLS])
    xbc = _dot(hn, win_ref[:, XBC_COLS])
    dt_raw = _dot(hn, wdt_ref[...])
    conv = convb_ref[...] + cs_ref[:, 0:CONV_DIM] * convw_ref[0:1, :]
    for k in range(1, CONV_K - 1):
        conv = conv + cs_ref[:, k * CONV_DIM:(k + 1) * CONV_DIM] * convw_ref[k:k + 1, :]
    conv = conv + xbc * convw_ref[CONV_K - 1:CONV_K, :]
    convo_ref[:, 0:(CONV_K - 2) * CONV_DIM] = cs_ref[:, CONV_DIM:(CONV_K - 1) * CONV_DIM]
    convo_ref[:, (CONV_K - 2) * CONV_DIM:] = xbc
    act = _silu(conv)
    xs = act[:, :D_INNER]
    xs_ref[...] = xs
    b_ref[...] = act[:, D_INNER:D_INNER + GROUPS * STATE]
    c_ref[...] = act[:, D_INNER + GROUPS * STATE:]
    dt = jax.nn.softplus(dt_raw + dtb_ref[...])
    dec = jnp.exp(dt * (-jnp.exp(alog_ref[...])))

    def widen(v):
        hi = v.astype(BF16)
        lo = (v - hi.astype(F32)).astype(BF16)
        return _dot(hi, expand_ref[...]) + _dot(lo, expand_ref[...])

    xdt_ref[...] = xs * widen(dt)
    dec_ref[...] = dec


def _ssd_sample_state_kernel(dec_ref, h_ref, xdt_ref, b_ref, c_ref, stacked_ref, ho_ref, y_ref):
    del stacked_ref
    phase = pl.program_id(0)

    @pl.when(phase != 0)
    def _():
        ho_ref[...] = jnp.zeros(ho_ref.shape, F32)

    @pl.when(phase == 0)
    def _():
        base = pl.program_id(1) * SAMPLE_BLOCK
        g_idx = lax.broadcasted_iota(jnp.int32, (GROUPS, D_INNER), 0)
        l_idx = lax.broadcasted_iota(jnp.int32, (GROUPS, D_INNER), 1)
        own = (l_idx // GROUP_WIDTH) == g_idx
        for bi in range(SAMPLE_BLOCK):
            u_t = jnp.where(own, xdt_ref[bi], 0.0).astype(BF16)
            new = lax.dot_general(u_t, b_ref[bi].astype(BF16), (((0,), (0,)), ((), ())),
                                  preferred_element_type=F32)
            parts = []
            for hd in range(HEADS):
                rows = slice(hd * HEAD_DIM, (hd + 1) * HEAD_DIM)
                parts.append(h_ref[bi, rows, :] * dec_ref[base + bi, hd] + new[rows])
            h_new = jnp.concatenate(parts, axis=0)
            ho_ref[bi] = h_new
            y_all = lax.dot_general(c_ref[bi].astype(BF16), h_new.astype(BF16),
                                    (((1,), (1,)), ((), ())), preferred_element_type=F32)
            y_ref[bi] = jnp.sum(jnp.where(own, y_all, 0.0), axis=0, keepdims=True)


def _ssd_sample_out_kernel(x_ref, y_ref, xs_ref, z_ref, dskip_ref, ng_ref, wout_ref, o_ref):
    y = y_ref[...] + xs_ref[...] * dskip_ref[...]
    z = z_ref[...]
    y = y * _silu(z)
    parts = []
    for g in range(GROUPS):
        yg = y[:, g * GROUP_WIDTH:(g + 1) * GROUP_WIDTH]
        parts.append(yg * lax.rsqrt(jnp.mean(yg * yg, axis=-1, keepdims=True) + RMS_EPS))
    yn = (jnp.concatenate(parts, axis=1) * ng_ref[...]).astype(BF16)
    o_ref[...] = x_ref[...] + _dot(yn, wout_ref[...])


def _ssd_sample(x, ssm_all, conv_all, ssm_new_all, layer, g, w_in, w_dt, conv_w, conv_b,
                dt_bias, a_log, expand, d_wide, norm_g, w_out):
    n = x.shape[0]
    n_layers = ssm_all.shape[0]
    gs = GROUPS * STATE
    full = lambda shape: pl.BlockSpec(shape, lambda i: (0,) * len(shape))
    z, xs, bm, cm, xdt, dec, conv_new = pl.pallas_call(
        _ssd_sample_in_kernel,
        out_shape=(jax.ShapeDtypeStruct((n, D_INNER), F32),
                   jax.ShapeDtypeStruct((n, D_INNER), F32),
                   jax.ShapeDtypeStruct((n, gs), F32),
                   jax.ShapeDtypeStruct((n, gs), F32),
                   jax.ShapeDtypeStruct((n, D_INNER), F32),
                   jax.ShapeDtypeStruct((n, LANES), F32),
                   jax.ShapeDtypeStruct((n, (CONV_K - 1) * CONV_DIM), F32)),
        grid=(1,),
        in_specs=[_resident((n, D_MODEL)),
                  _resident((1, D_MODEL)),
                  _layer_weight((D_MODEL, SSD_IN_MAIN), layer),
                  _layer_weight((D_MODEL, LANES), layer),
                  pl.BlockSpec((n, (CONV_K - 1) * CONV_DIM), lambda i: (layer, 0),
                               pipeline_mode=pl.Buffered(1)),
                  _resident((CONV_K, CONV_DIM)),
                  _resident((1, CONV_DIM)),
                  _resident((1, LANES)),
                  _resident((1, LANES)),
                  _resident((LANES, D_INNER))],
        out_specs=(full((n, D_INNER)), full((n, D_INNER)), full((n, gs)), full((n, gs)),
                   full((n, D_INNER)), full((n, LANES)), full((n, (CONV_K - 1) * CONV_DIM))),
        compiler_params=_params(1),
        name="ssd_sample_in",
    )(x, g, w_in, w_dt, conv_all.reshape(n_layers * n, (CONV_K - 1) * CONV_DIM), conv_w, conv_b,
      dt_bias, a_log, expand)

    blk = SAMPLE_BLOCK
    steps = n // blk
    first = ssm_new_all is None
    phases = n_layers - layer if first else 1
    held = lambda p, i: (jnp.where(p == 0, i, steps - 1), 0, 0)
    row = pl.BlockSpec((blk, 1, D_INNER), held)
    grp = pl.BlockSpec((blk, GROUPS, STATE), held)
    h_old = pl.BlockSpec((blk, D_INNER, STATE),
                         lambda p, i: (layer * steps + jnp.where(p == 0, i, steps - 1), 0, 0))
    h_new = pl.BlockSpec((blk, D_INNER, STATE), lambda p, i: ((layer + p) * steps + i, 0, 0))
    carried = ssm_all.reshape(n_layers * n, D_INNER, STATE)
    ssm_new_all, y = pl.pallas_call(
        _ssd_sample_state_kernel,
        out_shape=(jax.ShapeDtypeStruct((n_layers * n, D_INNER, STATE), F32),
                   jax.ShapeDtypeStruct((n, 1, D_INNER), F32)),
        grid=(phases, steps),
        in_specs=[pl.BlockSpec(memory_space=pltpu.SMEM), h_old, row, grp, grp,
                  pl.BlockSpec(memory_space=pl.ANY)],
        out_specs=(h_new, row),
        input_output_aliases={} if first else {5: 0},
        compiler_params=_params(2),
        name="ssd_sample_state",
    )(dec[:, :HEADS], carried, xdt.reshape(n, 1, D_INNER), bm.reshape(n, GROUPS, STATE),
      cm.reshape(n, GROUPS, STATE), carried if first else ssm_new_all)

    x_new = pl.pallas_call(
        _ssd_sample_out_kernel,
        out_shape=jax.ShapeDtypeStruct((n, D_MODEL), F32),
        grid=(1,),
        in_specs=[_resident((n, D_MODEL)), _resident((n, D_INNER)), _resident((n, D_INNER)),
                  _resident((n, D_INNER)), _resident((1, D_INNER)), _resident((1, D_INNER)),
                  _layer_weight((D_INNER, D_MODEL), layer)],
        out_specs=full((n, D_MODEL)),
        compiler_params=_params(1),
        name="ssd_sample_out",
    )(x, y.reshape(n, D_INNER), xs, z, d_wide, norm_g, w_out)
    return x_new, ssm_new_all, conv_new.reshape(n, CONV_K - 1, CONV_DIM)


def _pad_lanes(v):
    return jnp.pad(v.reshape(1, -1), ((0, 0), (0, LANES - v.shape[-1])))


def kernel(x_prompt, x_sample, state_ssm, state_conv, norm_mix_g, norm_mlp_g, norm_final_g,
           ssd_w_in, ssd_conv_w, ssd_conv_b, ssd_dt_bias, ssd_a_log, ssd_d, ssd_norm_g,
           ssd_w_out, cm_w_in, cm_b_in, cm_v_norm_g, cm_w_s, cm_b_s, cm_w_out,
           mlp_w_up, mlp_w_down):
    batch, seq, _ = x_prompt.shape
    dec_batch, dec_seq, _ = x_sample.shape
    assert dec_seq == 1 and seq % CHUNK == 0
    pos = PAST_LEN % CM_CHUNK

    xp = x_prompt.reshape(batch * seq, D_MODEL)
    xs = x_sample.reshape(dec_batch, D_MODEL)
    row = lambda v: v.reshape(1, -1)
    expand = jnp.asarray(np.kron(np.eye(LANES, HEADS, dtype=np.float32),
                                 np.ones((1, HEAD_DIM), np.float32)), BF16)

    w_ssd_in = ssd_w_in[:, :, :SSD_IN_MAIN].astype(BF16)
    w_ssd_dt = jnp.pad(ssd_w_in[:, :, SSD_IN_MAIN:].astype(BF16), ((0, 0), (0, 0), (0, LANES - HEADS)))
    w_ssd_out = ssd_w_out.astype(BF16)
    w_cm_in = cm_w_in.astype(BF16)
    w_cm_out = cm_w_out.astype(BF16)
    w_up = mlp_w_up.astype(BF16)
    w_down = mlp_w_down.astype(BF16)

    ssm_p, conv_p, conv_s, v_s = [], [], [], []
    ssm_s = None
    for i in range(DEPTH):
        j = i // 2
        g_mix = row(norm_mix_g[i])
        if i % 2 == 0:
            shared = (ssd_conv_w[j], row(ssd_conv_b[j]), _pad_lanes(ssd_dt_bias[j]),
                      _pad_lanes(ssd_a_log[j]))
            d_wide = jnp.repeat(ssd_d[j], HEAD_DIM).reshape(1, D_INNER)
            n_g = row(ssd_norm_g[j])
            xp, s_p, c_p = _ssd_prompt(xp, batch, j, g_mix, w_ssd_in, w_ssd_dt, *shared, d_wide, n_g,
                                       w_ssd_out)
            ssm_p.append(s_p.reshape(batch, HEADS, HEAD_DIM, STATE))
            conv_p.append(c_p)
            xs, ssm_s, c_s = _ssd_sample(xs, state_ssm, state_conv, ssm_s, j, g_mix, w_ssd_in, w_ssd_dt,
                                         *shared, expand, d_wide, n_g, w_ssd_out)
            conv_s.append(c_s)
        else:
            b_in = row(cm_b_in[j])
            v_g = row(cm_v_norm_g[j])
            b_s_wide = jnp.repeat(cm_b_s[j].T, CM_GROUP_DIM, axis=1)
            w_diag = jnp.repeat(cm_w_s[j][:, pos, pos], CM_GROUP_DIM).reshape(1, CM_WIDTH)
            b_diag = jnp.repeat(cm_b_s[j][:, pos], CM_GROUP_DIM).reshape(1, CM_WIDTH)
            xp, xs, v = _cm(xp, xs, j, g_mix, w_cm_in, b_in, v_g, cm_w_s, b_s_wide, w_diag, b_diag,
                            w_cm_out)
            v_s.append(v.reshape(dec_batch, dec_seq, CM_WIDTH))
        xp, xs = _mlp(xp, xs, i, row(norm_mlp_g[i]), w_up, w_down, row(norm_final_g), i == DEPTH - 1)

    return (xp.reshape(batch, seq, D_MODEL), xs.reshape(dec_batch, dec_seq, D_MODEL),
            jnp.stack(ssm_p), jnp.stack(conv_p),
            ssm_s.reshape(state_ssm.shape), jnp.stack(conv_s), jnp.stack(v_s))
```

```python
import functools

import numpy as np
import jax
import jax.numpy as jnp
from jax import lax
from jax.experimental import pallas as pl
from jax.experimental.pallas import tpu as pltpu

F32 = jnp.float32
BF16 = jnp.bfloat16

D_MODEL = 1024
DEPTH = 4
D_INNER = 2048
HEAD_DIM = 64
HEADS = 32
GROUPS = 8
HEADS_PER_GROUP = 4
STATE = 128
CONV_K = 4
CHUNK = 128
CONV_DIM = D_INNER + 2 * GROUPS * STATE
GROUP_WIDTH = D_INNER // GROUPS
CM_CHUNK = 128
CM_WIDTH = 2 * D_MODEL
CM_GROUPS = 8
CM_GROUP_DIM = CM_WIDTH // CM_GROUPS
D_FF = 4 * D_MODEL
PAST_LEN = 16384
RMS_EPS = 1e-5

LANES = 128
SUBLANES = 8
CHUNK_TILES = CHUNK // SUBLANES
WRAP_ROWS = (CONV_K - 1) * SUBLANES
CHUNK_EXT = WRAP_ROWS + CHUNK
VMEM_LIMIT = 56 * 1024 * 1024
SSD_ROWS = 512
CM_ROWS = 512
MLP_ROWS = 512
FF_BLOCK = 1024
COL_BLOCK = 512
SAMPLE_BLOCK = 4
SQRT_HALF = np.sqrt(0.5).astype(np.float32)
Z_COLS = slice(0, D_INNER)
XBC_COLS = slice(D_INNER, D_INNER + CONV_DIM)
SSD_IN_MAIN = D_INNER + CONV_DIM
SSD_IN_DIM = SSD_IN_MAIN + HEADS


def _rmsnorm(x, g):
    ms = jnp.mean(x * x, axis=-1, keepdims=True)
    return x * lax.rsqrt(ms + RMS_EPS) * g


def _gelu(x):
    return 0.5 * x * (1.0 + lax.erf(x * SQRT_HALF))


def _silu(x):
    h = 0.5 * x
    return h + h * jnp.tanh(h)


def _dot(a, b):
    return jnp.dot(a, b, preferred_element_type=F32)


def _resident(shape):
    nd = len(shape)
    return pl.BlockSpec(shape, lambda *_: (0,) * nd, pipeline_mode=pl.Buffered(1))


def _layer_weight(shape, layer):
    nd = len(shape)
    return pl.BlockSpec((None,) + tuple(shape), lambda *_: (layer,) + (0,) * nd,
                        pipeline_mode=pl.Buffered(1))


def _params(n_axes):
    return pltpu.CompilerParams(
        dimension_semantics=("arbitrary",) * n_axes, vmem_limit_bytes=VMEM_LIMIT)


def _mlp_kernel(xp_ref, xs_ref, g_ref, wup_ref, wdown_ref, gf_ref, op_ref, os_ref, *, final):
    def rows_through_mlp(x_ref, o_ref):
        x = x_ref[...]
        hn = _rmsnorm(x, g_ref[...]).astype(BF16)
        acc = x
        for c in range(D_FF // FF_BLOCK):
            cols = slice(c * FF_BLOCK, (c + 1) * FF_BLOCK)
            a = jnp.maximum(_dot(hn, wup_ref[:, cols]), 0.0)
            acc = acc + _dot((a * a).astype(BF16), wdown_ref[cols, :])
        if final:
            acc = _rmsnorm(acc, gf_ref[...])
        o_ref[...] = acc

    last = pl.num_programs(0) - 1
    pl.when(pl.program_id(0) < last)(functools.partial(rows_through_mlp, xp_ref, op_ref))
    pl.when(pl.program_id(0) == last)(functools.partial(rows_through_mlp, xs_ref, os_ref))


def _mlp(xp, xs, layer, g, w_up, w_down, g_final, final):
    n, ns = xp.shape[0], xs.shape[0]
    rows = min(MLP_ROWS, n)
    tiles = n // rows
    tile = pl.BlockSpec((rows, D_MODEL), lambda i: (jnp.minimum(i, tiles - 1), 0))
    return pl.pallas_call(
        functools.partial(_mlp_kernel, final=final),
        out_shape=(jax.ShapeDtypeStruct((n, D_MODEL), F32),
                   jax.ShapeDtypeStruct((ns, D_MODEL), F32)),
        grid=(tiles + 1,),
        in_specs=[tile,
                  _resident((ns, D_MODEL)),
                  _resident((1, D_MODEL)),
                  _layer_weight((D_MODEL, D_FF), layer),
                  _layer_weight((D_FF, D_MODEL), layer),
                  _resident((1, D_MODEL))],
        out_specs=(tile, pl.BlockSpec((ns, D_MODEL), lambda i: (0, 0))),
        compiler_params=_params(1),
        name="mlp",
    )(xp, xs, g, w_up, w_down, g_final)


def _cm_kernel(xp_ref, xs_ref, g_ref, win_ref, bin_ref, vg_ref, ws_ref, bs_ref, wdiag_ref, bdiag_ref,
               wout_ref, op_ref, os_ref, vs_ref, u_sc, v_sc, y_sc):
    last = pl.num_programs(0) - 1
    pl.when(pl.program_id(0) < last)(functools.partial(
        _cm_prompt_tile, xp_ref, g_ref, win_ref, bin_ref, vg_ref, ws_ref, bs_ref, wout_ref, op_ref,
        u_sc, v_sc, y_sc))
    pl.when(pl.program_id(0) == last)(functools.partial(
        _cm_sample_rows, xs_ref, g_ref, win_ref, bin_ref, vg_ref, wdiag_ref, bdiag_ref, wout_ref,
        os_ref, vs_ref))


def _cm_prompt_tile(x_ref, g_ref, win_ref, bin_ref, vg_ref, ws_ref, bs_ref, wout_ref,
                    o_ref, u_sc, v_sc, y_sc):
    rows = x_ref.shape[0]
    x = x_ref[...]
    hn = _rmsnorm(x, g_ref[...]).astype(BF16)
    for c in range(CM_WIDTH // COL_BLOCK):
        cols = slice(c * COL_BLOCK, (c + 1) * COL_BLOCK)
        u_sc[:, cols] = _gelu(_dot(hn, win_ref[:, cols]) + bin_ref[:, cols])
    ss = jnp.zeros((rows, 1), F32)
    for c in range(CM_WIDTH // COL_BLOCK):
        cols = slice(c * COL_BLOCK, (c + 1) * COL_BLOCK)
        wcols = slice(CM_WIDTH + c * COL_BLOCK, CM_WIDTH + (c + 1) * COL_BLOCK)
        gv = _gelu(_dot(hn, win_ref[:, wcols]) + bin_ref[:, wcols])
        ss = ss + jnp.sum(gv * gv, axis=-1, keepdims=True)
        v_sc[:, cols] = gv
    r = lax.rsqrt(ss * (1.0 / CM_WIDTH) + RMS_EPS)
    t_idx = lax.broadcasted_iota(jnp.int32, (CM_CHUNK, CM_CHUNK), 0)
    s_idx = lax.broadcasted_iota(jnp.int32, (CM_CHUNK, CM_CHUNK), 1)
    causal = t_idx >= s_idx
    for g in range(CM_GROUPS):
        cols = slice(g * CM_GROUP_DIM, (g + 1) * CM_GROUP_DIM)
        w = jnp.where(causal, ws_ref[g], 0.0).astype(BF16)
        for c in range(rows // CM_CHUNK):
            rs = slice(c * CM_CHUNK, (c + 1) * CM_CHUNK)
            v = (v_sc[rs, cols] * r[rs] * vg_ref[:, cols]).astype(BF16)
            s = _dot(w, v) + bs_ref[:, cols]
            y_sc[rs, cols] = (u_sc[rs, cols] * s).astype(BF16)
    o_ref[...] = x + _dot(y_sc[...], wout_ref[...])


def _cm_sample_rows(x_ref, g_ref, win_ref, bin_ref, vg_ref, wdiag_ref, bdiag_ref, wout_ref,
                    o_ref, v_ref):
    x = x_ref[...]
    hn = _rmsnorm(x, g_ref[...]).astype(BF16)
    u = _gelu(_dot(hn, win_ref[:, :CM_WIDTH]) + bin_ref[:, :CM_WIDTH])
    gv = _gelu(_dot(hn, win_ref[:, CM_WIDTH:]) + bin_ref[:, CM_WIDTH:])
    v = _rmsnorm(gv, vg_ref[...])
    v_ref[...] = v
    s = v * wdiag_ref[...] + bdiag_ref[...]
    o_ref[...] = x + _dot((u * s).astype(BF16), wout_ref[...])


def _cm(xp, xs, layer, g, w_in, b_in, v_g, w_s, b_s_wide, w_diag, b_diag, w_out):
    n, ns = xp.shape[0], xs.shape[0]
    rows = min(CM_ROWS, n)
    tiles = n // rows
    tile = pl.BlockSpec((rows, D_MODEL), lambda i: (jnp.minimum(i, tiles - 1), 0))
    whole = lambda width: pl.BlockSpec((ns, width), lambda i: (0, 0))
    return pl.pallas_call(
        _cm_kernel,
        out_shape=(jax.ShapeDtypeStruct((n, D_MODEL), F32),
                   jax.ShapeDtypeStruct((ns, D_MODEL), F32),
                   jax.ShapeDtypeStruct((ns, CM_WIDTH), F32)),
        grid=(tiles + 1,),
        in_specs=[tile,
                  _resident((ns, D_MODEL)),
                  _resident((1, D_MODEL)),
                  _layer_weight((D_MODEL, 2 * CM_WIDTH), layer),
                  _resident((1, 2 * CM_WIDTH)),
                  _resident((1, CM_WIDTH)),
                  _layer_weight((CM_GROUPS, CM_CHUNK, CM_CHUNK), layer),
                  _resident((CM_CHUNK, CM_WIDTH)),
                  _resident((1, CM_WIDTH)),
                  _resident((1, CM_WIDTH)),
                  _layer_weight((CM_WIDTH, D_MODEL), layer)],
        out_specs=(tile, whole(D_MODEL), whole(CM_WIDTH)),
        scratch_shapes=[pltpu.VMEM((rows, CM_WIDTH), F32),
                        pltpu.VMEM((rows, CM_WIDTH), F32),
                        pltpu.VMEM((rows, CM_WIDTH), BF16)],
        compiler_params=_params(1),
        name="cm",
    )(xp, xs, g, w_in, b_in, v_g, w_s, b_s_wide, w_diag, b_diag, w_out)


def _ssd_prompt_kernel(x_ref, g_ref, win_ref, wdt_ref, convw_ref, convb_ref, dtb_ref,
                       alog_ref, dskip_ref, ng_ref, wout_ref,
                       o_ref, ssm_ref, convo_ref,
                       ext_sc, carry_sc, z_sc, dt_sc, act_sc, yn_sc, ht_sc):
    rows = x_ref.shape[0]
    n_chunks = rows // CHUNK
    step = pl.program_id(1)

    @pl.when(step == 0)
    def _():
        carry_sc[...] = jnp.zeros(carry_sc.shape, F32)
        ht_sc[...] = jnp.zeros(ht_sc.shape, F32)

    def token_of(r):
        return (r % SUBLANES) * CHUNK_TILES + r // SUBLANES

    q_idx = lax.broadcasted_iota(jnp.int32, (CHUNK, CHUNK), 0)
    s_idx = lax.broadcasted_iota(jnp.int32, (CHUNK, CHUNK), 1)
    to_rows = (s_idx == token_of(q_idx)).astype(BF16)
    to_tokens = (q_idx == token_of(s_idx)).astype(BF16)
    causal = token_of(q_idx) >= token_of(s_idx)
    tril = causal.astype(F32)
    low = s_idx < HEAD_DIM

    hn = []
    for c in range(n_chunks):
        h = _rmsnorm(x_ref[c * CHUNK:(c + 1) * CHUNK, :], g_ref[...]).astype(BF16)
        hn.append(_dot(to_rows, h).astype(BF16))
    hn = jnp.concatenate(hn, axis=0)
    z_sc[...] = _dot(hn, win_ref[:, Z_COLS])
    dt_sc[...] = _dot(hn, wdt_ref[...])
    xbc = _dot(hn, win_ref[:, XBC_COLS])
    for c in range(n_chunks):
        ext_sc[c * CHUNK_EXT + WRAP_ROWS:(c + 1) * CHUNK_EXT, :] = xbc[c * CHUNK:(c + 1) * CHUNK]

    sub0 = lax.broadcasted_iota(jnp.int32, (WRAP_ROWS, COL_BLOCK), 0) % SUBLANES == 0
    for c in range(n_chunks):
        base = c * CHUNK_EXT
        for cblk in range(CONV_DIM // COL_BLOCK):
            cols = slice(cblk * COL_BLOCK, (cblk + 1) * COL_BLOCK)
            moved = jnp.concatenate(
                [pltpu.roll(ext_sc[base + CHUNK_EXT - WRAP_ROWS + m * SUBLANES:
                                   base + CHUNK_EXT - WRAP_ROWS + (m + 1) * SUBLANES, cols], 1, 0)
                 for m in range(CONV_K - 1)], axis=0)
            ext_sc[base:base + WRAP_ROWS, cols] = jnp.where(sub0, carry_sc[:, cols], moved)
            carry_sc[:, cols] = moved
            conv = convb_ref[:, cols] + ext_sc[base:base + CHUNK, cols] * convw_ref[0:1, cols]
            for k in range(1, CONV_K):
                conv = conv + (ext_sc[base + k * SUBLANES:base + k * SUBLANES + CHUNK, cols]
                               * convw_ref[k:k + 1, cols])
            act_sc[c * CHUNK:(c + 1) * CHUNK, cols] = _silu(conv)
    last = n_chunks * CHUNK_EXT
    for m in range(CONV_K - 1):
        r = last - (CONV_K - 1 - m) * SUBLANES + SUBLANES - 1
        convo_ref[0, m:m + 1, :] = ext_sc[r:r + 1, :]

    a_row = -jnp.exp(alog_ref[...])
    b_off = D_INNER
    c_off = D_INNER + GROUPS * STATE

    for c in range(n_chunks):
        rs = slice(c * CHUNK, (c + 1) * CHUNK)
        dt = jax.nn.softplus(dt_sc[rs, :] + dtb_ref[...])
        a = dt * a_row
        acum = jnp.dot(tril, a, precision=lax.Precision.HIGHEST, preferred_element_type=F32)
        acum_t = acum.T
        dt_t = dt.T
        w_t = jnp.exp(acum_t[:, CHUNK - 1:CHUNK] - acum_t) * dt_t
        cdec = jnp.exp(acum[CHUNK - 1:CHUNK, :])
        arow_t = acum_t - jnp.log(dt_t)
        for g in range(GROUPS):
            bg = act_sc[rs, b_off + g * STATE:b_off + (g + 1) * STATE]
            cg = act_sc[rs, c_off + g * STATE:c_off + (g + 1) * STATE].astype(BF16)
            bt = bg.T
            cb = _dot(cg, bt.astype(BF16))
            ht = ht_sc[g]
            y_off = _dot(cg, ht.astype(BF16))
            ys = []
            for pr in range(HEADS_PER_GROUP // 2):
                h0 = g * HEADS_PER_GROUP + 2 * pr
                lanes = slice(h0 * HEAD_DIM, (h0 + 2) * HEAD_DIM)
                half = slice(pr * LANES, (pr + 1) * LANES)
                xs = act_sc[rs, lanes]
                rhs = jnp.concatenate([jnp.where(low, xs, 0.0).astype(BF16),
                                       jnp.where(low, 0.0, xs).astype(BF16)], axis=0)
                ms, bs, cols_b = [], [], []
                for hd in (h0, h0 + 1):
                    col_b = jnp.broadcast_to(acum[:, hd:hd + 1], (CHUNK, CHUNK))
                    seg = col_b - arow_t[hd:hd + 1, :]
                    ms.append((cb * jnp.exp(jnp.where(causal, seg, -jnp.inf))).astype(BF16))
                    bs.append((bt * w_t[hd:hd + 1, :]).astype(BF16))
                    cols_b.append(col_b)
                lhs = jnp.concatenate([jnp.concatenate(ms, axis=1),
                                       jnp.concatenate(bs, axis=1)], axis=0)
                res = _dot(lhs, rhs)
                e = jnp.exp(jnp.where(low, cols_b[0], cols_b[1]))
                y = res[0:CHUNK] + e * y_off[:, half] + xs * dskip_ref[:, lanes]
                cd = jnp.where(low[0:1], cdec[:, h0:h0 + 1], cdec[:, h0 + 1:h0 + 2])
                ht_sc[g, :, half] = ht[:, half] * cd + res[CHUNK:2 * CHUNK]
                ys.append(y)
            gcols = slice(g * GROUP_WIDTH, (g + 1) * GROUP_WIDTH)
            yg = jnp.concatenate(ys, axis=1)
            zg = z_sc[rs, gcols]
            yg = yg * _silu(zg)
            yg = yg * lax.rsqrt(jnp.mean(yg * yg, axis=-1, keepdims=True) + RMS_EPS)
            yn_sc[rs, gcols] = (yg * ng_ref[:, gcols]).astype(BF16)
        yn = _dot(to_tokens, yn_sc[rs, :]).astype(BF16)
        o_ref[rs, :] = x_ref[rs, :] + _dot(yn, wout_ref[...])

    @pl.when(step == pl.num_programs(1) - 1)
    def _():
        for g in range(GROUPS):
            ssm_ref[0, g * GROUP_WIDTH:(g + 1) * GROUP_WIDTH, :] = ht_sc[g].T


def _ssd_prompt(x, batch, layer, g, w_in, w_dt, conv_w, conv_b, dt_bias, a_log, d_wide, norm_g, w_out):
    n = x.shape[0]
    seq = n // batch
    rows = min(SSD_ROWS, seq)
    steps = seq // rows
    return pl.pallas_call(
        _ssd_prompt_kernel,
        out_shape=(jax.ShapeDtypeStruct((n, D_MODEL), F32),
                   jax.ShapeDtypeStruct((batch, D_INNER, STATE), F32),
                   jax.ShapeDtypeStruct((batch, CONV_K - 1, CONV_DIM), F32)),
        grid=(batch, steps),
        in_specs=[pl.BlockSpec((rows, D_MODEL), lambda b, s: (b * steps + s, 0)),
                  _resident((1, D_MODEL)),
                  _layer_weight((D_MODEL, SSD_IN_DIM), layer),
                  _layer_weight((D_MODEL, LANES), layer),
                  _resident((CONV_K, CONV_DIM)),
                  _resident((1, CONV_DIM)),
                  _resident((1, LANES)),
                  _resident((1, LANES)),
                  _resident((1, D_INNER)),
                  _resident((1, D_INNER)),
                  _layer_weight((D_INNER, D_MODEL), layer)],
        out_specs=(pl.BlockSpec((rows, D_MODEL), lambda b, s: (b * steps + s, 0)),
                   pl.BlockSpec((1, D_INNER, STATE), lambda b, s: (b, 0, 0)),
                   pl.BlockSpec((1, CONV_K - 1, CONV_DIM), lambda b, s: (b, 0, 0))),
        scratch_shapes=[pltpu.VMEM((rows // CHUNK * CHUNK_EXT, CONV_DIM), F32),
                        pltpu.VMEM((WRAP_ROWS, CONV_DIM), F32),
                        pltpu.VMEM((rows, D_INNER), F32),
                        pltpu.VMEM((rows, LANES), F32),
                        pltpu.VMEM((rows, CONV_DIM), F32),
                        pltpu.VMEM((rows, D_INNER), BF16),
                        pltpu.VMEM((GROUPS, STATE, GROUP_WIDTH), F32)],
        compiler_params=_params(2),
        name="ssd_prompt",
    )(x, g, w_in, w_dt, conv_w, conv_b, dt_bias, a_log, d_wide, norm_g, w_out)


def _ssd_sample_in_kernel(x_ref, g_ref, win_ref, wdt_ref, cs_ref, convw_ref, convb_ref,
                          dtb_ref, alog_ref, expand_ref,
                          z_ref, xs_ref, b_ref, c_ref, xdt_ref, dec_ref, convo_ref):
    hn = _rmsnorm(x_ref[...], g_ref[...]).astype(BF16)
    z_ref[...] = _dot(hn, win_ref[:, Z_COLS])
    xbc = _dot(hn, win_ref[:, XBC_COLS])
    dt_raw = _dot(hn, wdt_ref[...])
    conv = convb_ref[...] + cs_ref[:, 0:CONV_DIM] * convw_ref[0:1, :]
    for k in range(1, CONV_K - 1):
        conv = conv + cs_ref[:, k * CONV_DIM:(k + 1) * CONV_DIM] * convw_ref[k:k + 1, :]
    conv = conv + xbc * convw_ref[CONV_K - 1:CONV_K, :]
    convo_ref[:, 0:(CONV_K - 2) * CONV_DIM] = cs_ref[:, CONV_DIM:(CONV_K - 1) * CONV_DIM]
    convo_ref[:, (CONV_K - 2) * CONV_DIM:] = xbc
    act = _silu(conv)
    xs = act[:, :D_INNER]
    xs_ref[...] = xs
    b_ref[...] = act[:, D_INNER:D_INNER + GROUPS * STATE]
    c_ref[...] = act[:, D_INNER + GROUPS * STATE:]
    dt = jax.nn.softplus(dt_raw + dtb_ref[...])
    dec = jnp.exp(dt * (-jnp.exp(alog_ref[...])))

    def widen(v):
        hi = v.astype(BF16)
        lo = (v - hi.astype(F32)).astype(BF16)
        return _dot(hi, expand_ref[...]) + _dot(lo, expand_ref[...])

    xdt_ref[...] = xs * widen(dt)
    dec_ref[...] = dec


def _ssd_sample_state_kernel(dec_ref, h_ref, xdt_ref, b_ref, c_ref, stacked_ref, ho_ref, y_ref):
    del stacked_ref
    phase = pl.program_id(0)

    @pl.when(phase != 0)
    def _():
        ho_ref[...] = jnp.zeros(ho_ref.shape, F32)

    @pl.when(phase == 0)
    def _():
        base = pl.program_id(1) * SAMPLE_BLOCK
        g_idx = lax.broadcasted_iota(jnp.int32, (GROUPS, D_INNER), 0)
        l_idx = lax.broadcasted_iota(jnp.int32, (GROUPS, D_INNER), 1)
        own = (l_idx // GROUP_WIDTH) == g_idx
        for bi in range(SAMPLE_BLOCK):
            u_t = jnp.where(own, xdt_ref[bi], 0.0).astype(BF16)
            new = lax.dot_general(u_t, b_ref[bi].astype(BF16), (((0,), (0,)), ((), ())),
                                  preferred_element_type=F32)
            parts = []
            for hd in range(HEADS):
                rows = slice(hd * HEAD_DIM, (hd + 1) * HEAD_DIM)
                parts.append(h_ref[bi, rows, :] * dec_ref[base + bi, hd] + new[rows])
            h_new = jnp.concatenate(parts, axis=0)
            ho_ref[bi] = h_new
            y_all = lax.dot_general(c_ref[bi].astype(BF16), h_new.astype(BF16),
                                    (((1,), (1,)), ((), ())), preferred_element_type=F32)
            y_ref[bi] = jnp.sum(jnp.where(own, y_all, 0.0), axis=0, keepdims=True)


def _ssd_sample_out_kernel(x_ref, y_ref, xs_ref, z_ref, dskip_ref, ng_ref, wout_ref, o_ref):
    y = y_ref[...] + xs_ref[...] * dskip_ref[...]
    z = z_ref[...]
    y = y * _silu(z)
    parts = []
    for g in range(GROUPS):
        yg = y[:, g * GROUP_WIDTH:(g + 1) * GROUP_WIDTH]
        parts.append(yg * lax.rsqrt(jnp.mean(yg * yg, axis=-1, keepdims=True) + RMS_EPS))
    yn = (jnp.concatenate(parts, axis=1) * ng_ref[...]).astype(BF16)
    o_ref[...] = x_ref[...] + _dot(yn, wout_ref[...])


def _ssd_sample(x, ssm_all, conv_all, ssm_new_all, layer, g, w_in, w_dt, conv_w, conv_b,
                dt_bias, a_log, expand, d_wide, norm_g, w_out):
    n = x.shape[0]
    n_layers = ssm_all.shape[0]
    gs = GROUPS * STATE
    full = lambda shape: pl.BlockSpec(shape, lambda i: (0,) * len(shape))
    z, xs, bm, cm, xdt, dec, conv_new = pl.pallas_call(
        _ssd_sample_in_kernel,
        out_shape=(jax.ShapeDtypeStruct((n, D_INNER), F32),
                   jax.ShapeDtypeStruct((n, D_INNER), F32),
                   jax.ShapeDtypeStruct((n, gs), F32),
                   jax.ShapeDtypeStruct((n, gs), F32),
                   jax.ShapeDtypeStruct((n, D_INNER), F32),
                   jax.ShapeDtypeStruct((n, LANES), F32),
                   jax.ShapeDtypeStruct((n, (CONV_K - 1) * CONV_DIM), F32)),
        grid=(1,),
        in_specs=[_resident((n, D_MODEL)),
                  _resident((1, D_MODEL)),
                  _layer_weight((D_MODEL, SSD_IN_DIM), layer),
                  _layer_weight((D_MODEL, LANES), layer),
                  pl.BlockSpec((n, (CONV_K - 1) * CONV_DIM), lambda i: (layer, 0),
                               pipeline_mode=pl.Buffered(1)),
                  _resident((CONV_K, CONV_DIM)),
                  _resident((1, CONV_DIM)),
                  _resident((1, LANES)),
                  _resident((1, LANES)),
                  _resident((LANES, D_INNER))],
        out_specs=(full((n, D_INNER)), full((n, D_INNER)), full((n, gs)), full((n, gs)),
                   full((n, D_INNER)), full((n, LANES)), full((n, (CONV_K - 1) * CONV_DIM))),
        compiler_params=_params(1),
        name="ssd_sample_in",
    )(x, g, w_in, w_dt, conv_all.reshape(n_layers * n, (CONV_K - 1) * CONV_DIM), conv_w, conv_b,
      dt_bias, a_log, expand)

    blk = SAMPLE_BLOCK
    steps = n // blk
    first = ssm_new_all is None
    phases = n_layers - layer if first else 1
    held = lambda p, i: (jnp.where(p == 0, i, steps - 1), 0, 0)
    row = pl.BlockSpec((blk, 1, D_INNER), held)
    grp = pl.BlockSpec((blk, GROUPS, STATE), held)
    h_old = pl.BlockSpec((blk, D_INNER, STATE),
                         lambda p, i: (layer * steps + jnp.where(p == 0, i, steps - 1), 0, 0))
    h_new = pl.BlockSpec((blk, D_INNER, STATE), lambda p, i: ((layer + p) * steps + i, 0, 0))
    carried = ssm_all.reshape(n_layers * n, D_INNER, STATE)
    ssm_new_all, y = pl.pallas_call(
        _ssd_sample_state_kernel,
        out_shape=(jax.ShapeDtypeStruct((n_layers * n, D_INNER, STATE), F32),
                   jax.ShapeDtypeStruct((n, 1, D_INNER), F32)),
        grid=(phases, steps),
        in_specs=[pl.BlockSpec(memory_space=pltpu.SMEM), h_old, row, grp, grp,
                  pl.BlockSpec(memory_space=pl.ANY)],
        out_specs=(h_new, row),
        input_output_aliases={} if first else {5: 0},
        compiler_params=_params(2),
        name="ssd_sample_state",
    )(dec[:, :HEADS], carried, xdt.reshape(n, 1, D_INNER), bm.reshape(n, GROUPS, STATE),
      cm.reshape(n, GROUPS, STATE), carried if first else ssm_new_all)

    x_new = pl.pallas_call(
        _ssd_sample_out_kernel,
        out_shape=jax.ShapeDtypeStruct((n, D_MODEL), F32),
        grid=(1,),
        in_specs=[_resident((n, D_MODEL)), _resident((n, D_INNER)), _resident((n, D_INNER)),
                  _resident((n, D_INNER)), _resident((1, D_INNER)), _resident((1, D_INNER)),
                  _layer_weight((D_INNER, D_MODEL), layer)],
        out_specs=full((n, D_MODEL)),
        compiler_params=_params(1),
        name="ssd_sample_out",
    )(x, y.reshape(n, D_INNER), xs, z, d_wide, norm_g, w_out)
    return x_new, ssm_new_all, conv_new.reshape(n, CONV_K - 1, CONV_DIM)


def _pad_lanes(v):
    return jnp.pad(v.reshape(1, -1), ((0, 0), (0, LANES - v.shape[-1])))


def kernel(x_prompt, x_sample, state_ssm, state_conv, norm_mix_g, norm_mlp_g, norm_final_g,
           ssd_w_in, ssd_conv_w, ssd_conv_b, ssd_dt_bias, ssd_a_log, ssd_d, ssd_norm_g,
           ssd_w_out, cm_w_in, cm_b_in, cm_v_norm_g, cm_w_s, cm_b_s, cm_w_out,
           mlp_w_up, mlp_w_down):
    batch, seq, _ = x_prompt.shape
    dec_batch, dec_seq, _ = x_sample.shape
    assert dec_seq == 1 and seq % CHUNK == 0
    pos = PAST_LEN % CM_CHUNK

    xp = x_prompt.reshape(batch * seq, D_MODEL)
    xs = x_sample.reshape(dec_batch, D_MODEL)
    row = lambda v: v.reshape(1, -1)
    expand = jnp.asarray(np.kron(np.eye(LANES, HEADS, dtype=np.float32),
                                 np.ones((1, HEAD_DIM), np.float32)), BF16)

    w_ssd_in = ssd_w_in.astype(BF16)
    w_ssd_dt = jnp.pad(ssd_w_in[:, :, SSD_IN_MAIN:].astype(BF16), ((0, 0), (0, 0), (0, LANES - HEADS)))
    w_ssd_out = ssd_w_out.astype(BF16)
    w_cm_in = cm_w_in.astype(BF16)
    w_cm_out = cm_w_out.astype(BF16)
    w_up = mlp_w_up.astype(BF16)
    w_down = mlp_w_down.astype(BF16)

    ssm_p, conv_p, conv_s, v_s = [], [], [], []
    ssm_s = None
    for i in range(DEPTH):
        j = i // 2
        g_mix = row(norm_mix_g[i])
        if i % 2 == 0:
            shared = (ssd_conv_w[j], row(ssd_conv_b[j]), _pad_lanes(ssd_dt_bias[j]),
                      _pad_lanes(ssd_a_log[j]))
            d_wide = jnp.repeat(ssd_d[j], HEAD_DIM).reshape(1, D_INNER)
            n_g = row(ssd_norm_g[j])
            xp, s_p, c_p = _ssd_prompt(xp, batch, j, g_mix, w_ssd_in, w_ssd_dt, *shared, d_wide, n_g,
                                       w_ssd_out)
            ssm_p.append(s_p.reshape(batch, HEADS, HEAD_DIM, STATE))
            conv_p.append(c_p)
            xs, ssm_s, c_s = _ssd_sample(xs, state_ssm, state_conv, ssm_s, j, g_mix, w_ssd_in, w_ssd_dt,
                                         *shared, expand, d_wide, n_g, w_ssd_out)
            conv_s.append(c_s)
        else:
            b_in = row(cm_b_in[j])
            v_g = row(cm_v_norm_g[j])
            b_s_wide = jnp.repeat(cm_b_s[j].T, CM_GROUP_DIM, axis=1)
            w_diag = jnp.repeat(cm_w_s[j][:, pos, pos], CM_GROUP_DIM).reshape(1, CM_WIDTH)
            b_diag = jnp.repeat(cm_b_s[j][:, pos], CM_GROUP_DIM).reshape(1, CM_WIDTH)
            xp, xs, v = _cm(xp, xs, j, g_mix, w_cm_in, b_in, v_g, cm_w_s, b_s_wide, w_diag, b_diag,
                            w_cm_out)
            v_s.append(v.reshape(dec_batch, dec_seq, CM_WIDTH))
        xp, xs = _mlp(xp, xs, i, row(norm_mlp_g[i]), w_up, w_down, row(norm_final_g), i == DEPTH - 1)

    return (xp.reshape(batch, seq, D_MODEL), xs.reshape(dec_batch, dec_seq, D_MODEL),
            jnp.stack(ssm_p), jnp.stack(conv_p),
            ssm_s.reshape(state_ssm.shape), jnp.stack(conv_s), jnp.stack(v_s))
```

```python
import functools

import numpy as np
import jax
import jax.numpy as jnp
from jax import lax
from jax.experimental import pallas as pl
from jax.experimental.pallas import tpu as pltpu

F32 = jnp.float32
BF16 = jnp.bfloat16

D_MODEL = 1024
DEPTH = 4
D_INNER = 2048
HEAD_DIM = 64
HEADS = 32
GROUPS = 8
HEADS_PER_GROUP = 4
STATE = 128
CONV_K = 4
CHUNK = 128
CONV_DIM = D_INNER + 2 * GROUPS * STATE
GROUP_WIDTH = D_INNER // GROUPS
CM_CHUNK = 128
CM_WIDTH = 2 * D_MODEL
CM_GROUPS = 8
CM_GROUP_DIM = CM_WIDTH // CM_GROUPS
D_FF = 4 * D_MODEL
PAST_LEN = 16384
RMS_EPS = 1e-5

LANES = 128
SUBLANES = 8
CHUNK_TILES = CHUNK // SUBLANES
WRAP_ROWS = (CONV_K - 1) * SUBLANES
CHUNK_EXT = WRAP_ROWS + CHUNK
VMEM_LIMIT = 56 * 1024 * 1024
SSD_ROWS = 512
CM_ROWS = 1024
MLP_ROWS = 1024
FF_BLOCK = 1024
COL_BLOCK = 512
SAMPLE_BLOCK = 8
SQRT_HALF = np.sqrt(0.5).astype(np.float32)
Z_COLS = slice(0, D_INNER)
XBC_COLS = slice(D_INNER, D_INNER + CONV_DIM)
SSD_IN_MAIN = D_INNER + CONV_DIM
SSD_IN_DIM = SSD_IN_MAIN + HEADS


def _rmsnorm(x, g):
    ms = jnp.mean(x * x, axis=-1, keepdims=True)
    return x * lax.rsqrt(ms + RMS_EPS) * g


def _gelu(x):
    return 0.5 * x * (1.0 + lax.erf(x * SQRT_HALF))


def _silu(x):
    h = 0.5 * x
    return h + h * jnp.tanh(h)


def _dot(a, b):
    return jnp.dot(a, b, preferred_element_type=F32)


def _resident(shape):
    nd = len(shape)
    return pl.BlockSpec(shape, lambda *_: (0,) * nd, pipeline_mode=pl.Buffered(1))


def _layer_weight(shape, layer):
    nd = len(shape)
    return pl.BlockSpec((None,) + tuple(shape), lambda *_: (layer,) + (0,) * nd,
                        pipeline_mode=pl.Buffered(1))


def _params(n_axes):
    return pltpu.CompilerParams(
        dimension_semantics=("arbitrary",) * n_axes, vmem_limit_bytes=VMEM_LIMIT)


def _mlp_kernel(xp_ref, xs_ref, g_ref, wup_ref, wdown_ref, gf_ref, op_ref, os_ref, *, final):
    def rows_through_mlp(x_ref, o_ref):
        x = x_ref[...]
        hn = _rmsnorm(x, g_ref[...]).astype(BF16)
        acc = x
        for c in range(D_FF // FF_BLOCK):
            cols = slice(c * FF_BLOCK, (c + 1) * FF_BLOCK)
            a = jnp.maximum(_dot(hn, wup_ref[:, cols]), 0.0)
            acc = acc + _dot((a * a).astype(BF16), wdown_ref[cols, :])
        if final:
            acc = _rmsnorm(acc, gf_ref[...])
        o_ref[...] = acc

    last = pl.num_programs(0) - 1
    pl.when(pl.program_id(0) < last)(functools.partial(rows_through_mlp, xp_ref, op_ref))
    pl.when(pl.program_id(0) == last)(functools.partial(rows_through_mlp, xs_ref, os_ref))


def _mlp(xp, xs, layer, g, w_up, w_down, g_final, final):
    n, ns = xp.shape[0], xs.shape[0]
    rows = min(MLP_ROWS, n)
    tiles = n // rows
    tile = pl.BlockSpec((rows, D_MODEL), lambda i: (jnp.minimum(i, tiles - 1), 0))
    return pl.pallas_call(
        functools.partial(_mlp_kernel, final=final),
        out_shape=(jax.ShapeDtypeStruct((n, D_MODEL), F32),
                   jax.ShapeDtypeStruct((ns, D_MODEL), F32)),
        grid=(tiles + 1,),
        in_specs=[tile,
                  _resident((ns, D_MODEL)),
                  _resident((1, D_MODEL)),
                  _layer_weight((D_MODEL, D_FF), layer),
                  _layer_weight((D_FF, D_MODEL), layer),
                  _resident((1, D_MODEL))],
        out_specs=(tile, pl.BlockSpec((ns, D_MODEL), lambda i: (0, 0))),
        compiler_params=_params(1),
        name="mlp",
    )(xp, xs, g, w_up, w_down, g_final)


def _cm_kernel(xp_ref, xs_ref, g_ref, win_ref, bin_ref, vg_ref, ws_ref, bs_ref, wdiag_ref, bdiag_ref,
               wout_ref, op_ref, os_ref, vs_ref, u_sc, v_sc, y_sc):
    last = pl.num_programs(0) - 1
    pl.when(pl.program_id(0) < last)(functools.partial(
        _cm_prompt_tile, xp_ref, g_ref, win_ref, bin_ref, vg_ref, ws_ref, bs_ref, wout_ref, op_ref,
        u_sc, v_sc, y_sc))
    pl.when(pl.program_id(0) == last)(functools.partial(
        _cm_sample_rows, xs_ref, g_ref, win_ref, bin_ref, vg_ref, wdiag_ref, bdiag_ref, wout_ref,
        os_ref, vs_ref))


def _cm_prompt_tile(x_ref, g_ref, win_ref, bin_ref, vg_ref, ws_ref, bs_ref, wout_ref,
                    o_ref, u_sc, v_sc, y_sc):
    rows = x_ref.shape[0]
    x = x_ref[...]
    hn = _rmsnorm(x, g_ref[...]).astype(BF16)
    for c in range(CM_WIDTH // COL_BLOCK):
        cols = slice(c * COL_BLOCK, (c + 1) * COL_BLOCK)
        u_sc[:, cols] = _gelu(_dot(hn, win_ref[:, cols]) + bin_ref[:, cols])
    ss = jnp.zeros((rows, 1), F32)
    for c in range(CM_WIDTH // COL_BLOCK):
        cols = slice(c * COL_BLOCK, (c + 1) * COL_BLOCK)
        wcols = slice(CM_WIDTH + c * COL_BLOCK, CM_WIDTH + (c + 1) * COL_BLOCK)
        gv = _gelu(_dot(hn, win_ref[:, wcols]) + bin_ref[:, wcols])
        ss = ss + jnp.sum(gv * gv, axis=-1, keepdims=True)
        v_sc[:, cols] = gv
    r = lax.rsqrt(ss * (1.0 / CM_WIDTH) + RMS_EPS)
    t_idx = lax.broadcasted_iota(jnp.int32, (CM_CHUNK, CM_CHUNK), 0)
    s_idx = lax.broadcasted_iota(jnp.int32, (CM_CHUNK, CM_CHUNK), 1)
    causal = t_idx >= s_idx
    for g in range(CM_GROUPS):
        cols = slice(g * CM_GROUP_DIM, (g + 1) * CM_GROUP_DIM)
        w = jnp.where(causal, ws_ref[g], 0.0).astype(BF16)
        for c in range(rows // CM_CHUNK):
            rs = slice(c * CM_CHUNK, (c + 1) * CM_CHUNK)
            v = (v_sc[rs, cols] * r[rs] * vg_ref[:, cols]).astype(BF16)
            s = _dot(w, v) + bs_ref[:, cols]
            y_sc[rs, cols] = (u_sc[rs, cols] * s).astype(BF16)
    o_ref[...] = x + _dot(y_sc[...], wout_ref[...])


def _cm_sample_rows(x_ref, g_ref, win_ref, bin_ref, vg_ref, wdiag_ref, bdiag_ref, wout_ref,
                    o_ref, v_ref):
    x = x_ref[...]
    hn = _rmsnorm(x, g_ref[...]).astype(BF16)
    u = _gelu(_dot(hn, win_ref[:, :CM_WIDTH]) + bin_ref[:, :CM_WIDTH])
    gv = _gelu(_dot(hn, win_ref[:, CM_WIDTH:]) + bin_ref[:, CM_WIDTH:])
    v = _rmsnorm(gv, vg_ref[...])
    v_ref[...] = v
    s = v * wdiag_ref[...] + bdiag_ref[...]
    o_ref[...] = x + _dot((u * s).astype(BF16), wout_ref[...])


def _cm(xp, xs, layer, g, w_in, b_in, v_g, w_s, b_s_wide, w_diag, b_diag, w_out):
    n, ns = xp.shape[0], xs.shape[0]
    rows = min(CM_ROWS, n)
    tiles = n // rows
    tile = pl.BlockSpec((rows, D_MODEL), lambda i: (jnp.minimum(i, tiles - 1), 0))
    whole = lambda width: pl.BlockSpec((ns, width), lambda i: (0, 0))
    return pl.pallas_call(
        _cm_kernel,
        out_shape=(jax.ShapeDtypeStruct((n, D_MODEL), F32),
                   jax.ShapeDtypeStruct((ns, D_MODEL), F32),
                   jax.ShapeDtypeStruct((ns, CM_WIDTH), F32)),
        grid=(tiles + 1,),
        in_specs=[tile,
                  _resident((ns, D_MODEL)),
                  _resident((1, D_MODEL)),
                  _layer_weight((D_MODEL, 2 * CM_WIDTH), layer),
                  _resident((1, 2 * CM_WIDTH)),
                  _resident((1, CM_WIDTH)),
                  _layer_weight((CM_GROUPS, CM_CHUNK, CM_CHUNK), layer),
                  _resident((CM_CHUNK, CM_WIDTH)),
                  _resident((1, CM_WIDTH)),
                  _resident((1, CM_WIDTH)),
                  _layer_weight((CM_WIDTH, D_MODEL), layer)],
        out_specs=(tile, whole(D_MODEL), whole(CM_WIDTH)),
        scratch_shapes=[pltpu.VMEM((rows, CM_WIDTH), F32),
                        pltpu.VMEM((rows, CM_WIDTH), F32),
                        pltpu.VMEM((rows, CM_WIDTH), BF16)],
        compiler_params=_params(1),
        name="cm",
    )(xp, xs, g, w_in, b_in, v_g, w_s, b_s_wide, w_diag, b_diag, w_out)


def _ssd_prompt_kernel(x_ref, g_ref, win_ref, wdt_ref, convw_ref, convb_ref, dtb_ref,
                       alog_ref, dskip_ref, ng_ref, wout_ref,
                       o_ref, ssm_ref, convo_ref,
                       ext_sc, carry_sc, z_sc, dt_sc, act_sc, yn_sc, ht_sc):
    rows = x_ref.shape[0]
    n_chunks = rows // CHUNK
    step = pl.program_id(1)

    @pl.when(step == 0)
    def _():
        carry_sc[...] = jnp.zeros(carry_sc.shape, F32)
        ht_sc[...] = jnp.zeros(ht_sc.shape, F32)

    def token_of(r):
        return (r % SUBLANES) * CHUNK_TILES + r // SUBLANES

    q_idx = lax.broadcasted_iota(jnp.int32, (CHUNK, CHUNK), 0)
    s_idx = lax.broadcasted_iota(jnp.int32, (CHUNK, CHUNK), 1)
    to_rows = (s_idx == token_of(q_idx)).astype(BF16)
    to_tokens = (q_idx == token_of(s_idx)).astype(BF16)
    causal = token_of(q_idx) >= token_of(s_idx)
    tril = causal.astype(F32)
    low = s_idx < HEAD_DIM

    hn = []
    for c in range(n_chunks):
        h = _rmsnorm(x_ref[c * CHUNK:(c + 1) * CHUNK, :], g_ref[...]).astype(BF16)
        hn.append(_dot(to_rows, h).astype(BF16))
    hn = jnp.concatenate(hn, axis=0)
    z_sc[...] = _dot(hn, win_ref[:, Z_COLS])
    dt_sc[...] = _dot(hn, wdt_ref[...])
    xbc = _dot(hn, win_ref[:, XBC_COLS])
    for c in range(n_chunks):
        ext_sc[c * CHUNK_EXT + WRAP_ROWS:(c + 1) * CHUNK_EXT, :] = xbc[c * CHUNK:(c + 1) * CHUNK]

    sub0 = lax.broadcasted_iota(jnp.int32, (WRAP_ROWS, COL_BLOCK), 0) % SUBLANES == 0
    for c in range(n_chunks):
        base = c * CHUNK_EXT
        for cblk in range(CONV_DIM // COL_BLOCK):
            cols = slice(cblk * COL_BLOCK, (cblk + 1) * COL_BLOCK)
            moved = jnp.concatenate(
                [pltpu.roll(ext_sc[base + CHUNK_EXT - WRAP_ROWS + m * SUBLANES:
                                   base + CHUNK_EXT - WRAP_ROWS + (m + 1) * SUBLANES, cols], 1, 0)
                 for m in range(CONV_K - 1)], axis=0)
            ext_sc[base:base + WRAP_ROWS, cols] = jnp.where(sub0, carry_sc[:, cols], moved)
            carry_sc[:, cols] = moved
            conv = convb_ref[:, cols] + ext_sc[base:base + CHUNK, cols] * convw_ref[0:1, cols]
            for k in range(1, CONV_K):
                conv = conv + (ext_sc[base + k * SUBLANES:base + k * SUBLANES + CHUNK, cols]
                               * convw_ref[k:k + 1, cols])
            act_sc[c * CHUNK:(c + 1) * CHUNK, cols] = _silu(conv)
    last = n_chunks * CHUNK_EXT
    for m in range(CONV_K - 1):
        r = last - (CONV_K - 1 - m) * SUBLANES + SUBLANES - 1
        convo_ref[0, m:m + 1, :] = ext_sc[r:r + 1, :]

    a_row = -jnp.exp(alog_ref[...])
    b_off = D_INNER
    c_off = D_INNER + GROUPS * STATE

    for c in range(n_chunks):
        rs = slice(c * CHUNK, (c + 1) * CHUNK)
        dt = jax.nn.softplus(dt_sc[rs, :] + dtb_ref[...])
        a = dt * a_row
        acum = jnp.dot(tril, a, precision=lax.Precision.HIGHEST, preferred_element_type=F32)
        acum_t = acum.T
        dt_t = dt.T
        w_t = jnp.exp(acum_t[:, CHUNK - 1:CHUNK] - acum_t) * dt_t
        cdec = jnp.exp(acum[CHUNK - 1:CHUNK, :])
        arow_t = acum_t - jnp.log(dt_t)
        for g in range(GROUPS):
            bg = act_sc[rs, b_off + g * STATE:b_off + (g + 1) * STATE]
            cg = act_sc[rs, c_off + g * STATE:c_off + (g + 1) * STATE].astype(BF16)
            bt = bg.T
            cb = _dot(cg, bt.astype(BF16))
            ht = ht_sc[g]
            y_off = _dot(cg, ht.astype(BF16))
            ys = []
            for pr in range(HEADS_PER_GROUP // 2):
                h0 = g * HEADS_PER_GROUP + 2 * pr
                lanes = slice(h0 * HEAD_DIM, (h0 + 2) * HEAD_DIM)
                half = slice(pr * LANES, (pr + 1) * LANES)
                xs = act_sc[rs, lanes]
                rhs = jnp.concatenate([jnp.where(low, xs, 0.0).astype(BF16),
                                       jnp.where(low, 0.0, xs).astype(BF16)], axis=0)
                ms, bs, cols_b = [], [], []
                for hd in (h0, h0 + 1):
                    col_b = jnp.broadcast_to(acum[:, hd:hd + 1], (CHUNK, CHUNK))
                    seg = col_b - arow_t[hd:hd + 1, :]
                    ms.append((cb * jnp.exp(jnp.where(causal, seg, -jnp.inf))).astype(BF16))
                    bs.append((bt * w_t[hd:hd + 1, :]).astype(BF16))
                    cols_b.append(col_b)
                lhs = jnp.concatenate([jnp.concatenate(ms, axis=1),
                                       jnp.concatenate(bs, axis=1)], axis=0)
                res = _dot(lhs, rhs)
                e = jnp.exp(jnp.where(low, cols_b[0], cols_b[1]))
                y = res[0:CHUNK] + e * y_off[:, half] + xs * dskip_ref[:, lanes]
                cd = jnp.where(low[0:1], cdec[:, h0:h0 + 1], cdec[:, h0 + 1:h0 + 2])
                ht_sc[g, :, half] = ht[:, half] * cd + res[CHUNK:2 * CHUNK]
                ys.append(y)
            gcols = slice(g * GROUP_WIDTH, (g + 1) * GROUP_WIDTH)
            yg = jnp.concatenate(ys, axis=1)
            zg = z_sc[rs, gcols]
            yg = yg * _silu(zg)
            yg = yg * lax.rsqrt(jnp.mean(yg * yg, axis=-1, keepdims=True) + RMS_EPS)
            yn_sc[rs, gcols] = (yg * ng_ref[:, gcols]).astype(BF16)
        yn = _dot(to_tokens, yn_sc[rs, :]).astype(BF16)
        o_ref[rs, :] = x_ref[rs, :] + _dot(yn, wout_ref[...])

    @pl.when(step == pl.num_programs(1) - 1)
    def _():
        for g in range(GROUPS):
            ssm_ref[0, g * GROUP_WIDTH:(g + 1) * GROUP_WIDTH, :] = ht_sc[g].T


def _ssd_prompt(x, batch, layer, g, w_in, w_dt, conv_w, conv_b, dt_bias, a_log, d_wide, norm_g, w_out):
    n = x.shape[0]
    seq = n // batch
    rows = min(SSD_ROWS, seq)
    steps = seq // rows
    return pl.pallas_call(
        _ssd_prompt_kernel,
        out_shape=(jax.ShapeDtypeStruct((n, D_MODEL), F32),
                   jax.ShapeDtypeStruct((batch, D_INNER, STATE), F32),
                   jax.ShapeDtypeStruct((batch, CONV_K - 1, CONV_DIM), F32)),
        grid=(batch, steps),
        in_specs=[pl.BlockSpec((rows, D_MODEL), lambda b, s: (b * steps + s, 0)),
                  _resident((1, D_MODEL)),
                  _layer_weight((D_MODEL, SSD_IN_DIM), layer),
                  _layer_weight((D_MODEL, LANES), layer),
                  _resident((CONV_K, CONV_DIM)),
                  _resident((1, CONV_DIM)),
                  _resident((1, LANES)),
                  _resident((1, LANES)),
                  _resident((1, D_INNER)),
                  _resident((1, D_INNER)),
                  _layer_weight((D_INNER, D_MODEL), layer)],
        out_specs=(pl.BlockSpec((rows, D_MODEL), lambda b, s: (b * steps + s, 0)),
                   pl.BlockSpec((1, D_INNER, STATE), lambda b, s: (b, 0, 0)),
                   pl.BlockSpec((1, CONV_K - 1, CONV_DIM), lambda b, s: (b, 0, 0))),
        scratch_shapes=[pltpu.VMEM((rows // CHUNK * CHUNK_EXT, CONV_DIM), F32),
                        pltpu.VMEM((WRAP_ROWS, CONV_DIM), F32),
                        pltpu.VMEM((rows, D_INNER), F32),
                        pltpu.VMEM((rows, LANES), F32),
                        pltpu.VMEM((rows, CONV_DIM), F32),
                        pltpu.VMEM((rows, D_INNER), BF16),
                        pltpu.VMEM((GROUPS, STATE, GROUP_WIDTH), F32)],
        compiler_params=_params(2),
        name="ssd_prompt",
    )(x, g, w_in, w_dt, conv_w, conv_b, dt_bias, a_log, d_wide, norm_g, w_out)


def _ssd_sample_in_kernel(x_ref, g_ref, win_ref, wdt_ref, cs_ref, convw_ref, convb_ref,
                          dtb_ref, alog_ref, expand_ref,
                          z_ref, xs_ref, b_ref, c_ref, xdt_ref, dec_ref, convo_ref):
    hn = _rmsnorm(x_ref[...], g_ref[...]).astype(BF16)
    z_ref[...] = _dot(hn, win_ref[:, Z_COLS])
    xbc = _dot(hn, win_ref[:, XBC_COLS])
    dt_raw = _dot(hn, wdt_ref[...])
    conv = convb_ref[...] + cs_ref[:, 0:CONV_DIM] * convw_ref[0:1, :]
    for k in range(1, CONV_K - 1):
        conv = conv + cs_ref[:, k * CONV_DIM:(k + 1) * CONV_DIM] * convw_ref[k:k + 1, :]
    conv = conv + xbc * convw_ref[CONV_K - 1:CONV_K, :]
    convo_ref[:, 0:(CONV_K - 2) * CONV_DIM] = cs_ref[:, CONV_DIM:(CONV_K - 1) * CONV_DIM]
    convo_ref[:, (CONV_K - 2) * CONV_DIM:] = xbc
    act = _silu(conv)
    xs = act[:, :D_INNER]
    xs_ref[...] = xs
    b_ref[...] = act[:, D_INNER:D_INNER + GROUPS * STATE]
    c_ref[...] = act[:, D_INNER + GROUPS * STATE:]
    dt = jax.nn.softplus(dt_raw + dtb_ref[...])
    dec = jnp.exp(dt * (-jnp.exp(alog_ref[...])))

    def widen(v):
        hi = v.astype(BF16)
        lo = (v - hi.astype(F32)).astype(BF16)
        return _dot(hi, expand_ref[...]) + _dot(lo, expand_ref[...])

    xdt_ref[...] = xs * widen(dt)
    dec_ref[...] = dec


def _ssd_sample_state_kernel(dec_ref, h_ref, xdt_ref, b_ref, c_ref, stacked_ref, ho_ref, y_ref):
    del stacked_ref
    phase = pl.program_id(0)

    @pl.when(phase != 0)
    def _():
        ho_ref[...] = jnp.zeros(ho_ref.shape, F32)

    @pl.when(phase == 0)
    def _():
        base = pl.program_id(1) * SAMPLE_BLOCK
        g_idx = lax.broadcasted_iota(jnp.int32, (GROUPS, D_INNER), 0)
        l_idx = lax.broadcasted_iota(jnp.int32, (GROUPS, D_INNER), 1)
        own = (l_idx // GROUP_WIDTH) == g_idx
        for bi in range(SAMPLE_BLOCK):
            u_t = jnp.where(own, xdt_ref[bi], 0.0).astype(BF16)
            new = lax.dot_general(u_t, b_ref[bi].astype(BF16), (((0,), (0,)), ((), ())),
                                  preferred_element_type=F32)
            parts = []
            for hd in range(HEADS):
                rows = slice(hd * HEAD_DIM, (hd + 1) * HEAD_DIM)
                parts.append(h_ref[bi, rows, :] * dec_ref[base + bi, hd] + new[rows])
            h_new = jnp.concatenate(parts, axis=0)
            ho_ref[bi] = h_new
            y_all = lax.dot_general(c_ref[bi].astype(BF16), h_new.astype(BF16),
                                    (((1,), (1,)), ((), ())), preferred_element_type=F32)
            y_ref[bi] = jnp.sum(jnp.where(own, y_all, 0.0), axis=0, keepdims=True)


def _ssd_sample_out_kernel(x_ref, y_ref, xs_ref, z_ref, dskip_ref, ng_ref, wout_ref, o_ref):
    y = y_ref[...] + xs_ref[...] * dskip_ref[...]
    z = z_ref[...]
    y = y * _silu(z)
    parts = []
    for g in range(GROUPS):
        yg = y[:, g * GROUP_WIDTH:(g + 1) * GROUP_WIDTH]
        parts.append(yg * lax.rsqrt(jnp.mean(yg * yg, axis=-1, keepdims=True) + RMS_EPS))
    yn = (jnp.concatenate(parts, axis=1) * ng_ref[...]).astype(BF16)
    o_ref[...] = x_ref[...] + _dot(yn, wout_ref[...])


def _ssd_sample(x, ssm_all, conv_all, ssm_new_all, layer, g, w_in, w_dt, conv_w, conv_b,
                dt_bias, a_log, expand, d_wide, norm_g, w_out):
    n = x.shape[0]
    n_layers = ssm_all.shape[0]
    gs = GROUPS * STATE
    full = lambda shape: pl.BlockSpec(shape, lambda i: (0,) * len(shape))
    z, xs, bm, cm, xdt, dec, conv_new = pl.pallas_call(
        _ssd_sample_in_kernel,
        out_shape=(jax.ShapeDtypeStruct((n, D_INNER), F32),
                   jax.ShapeDtypeStruct((n, D_INNER), F32),
                   jax.ShapeDtypeStruct((n, gs), F32),
                   jax.ShapeDtypeStruct((n, gs), F32),
                   jax.ShapeDtypeStruct((n, D_INNER), F32),
                   jax.ShapeDtypeStruct((n, LANES), F32),
                   jax.ShapeDtypeStruct((n, (CONV_K - 1) * CONV_DIM), F32)),
        grid=(1,),
        in_specs=[_resident((n, D_MODEL)),
                  _resident((1, D_MODEL)),
                  _layer_weight((D_MODEL, SSD_IN_DIM), layer),
                  _layer_weight((D_MODEL, LANES), layer),
                  pl.BlockSpec((n, (CONV_K - 1) * CONV_DIM), lambda i: (layer, 0),
                               pipeline_mode=pl.Buffered(1)),
                  _resident((CONV_K, CONV_DIM)),
                  _resident((1, CONV_DIM)),
                  _resident((1, LANES)),
                  _resident((1, LANES)),
                  _resident((LANES, D_INNER))],
        out_specs=(full((n, D_INNER)), full((n, D_INNER)), full((n, gs)), full((n, gs)),
                   full((n, D_INNER)), full((n, LANES)), full((n, (CONV_K - 1) * CONV_DIM))),
        compiler_params=_params(1),
        name="ssd_sample_in",
    )(x, g, w_in, w_dt, conv_all.reshape(n_layers * n, (CONV_K - 1) * CONV_DIM), conv_w, conv_b,
      dt_bias, a_log, expand)

    blk = SAMPLE_BLOCK
    steps = n // blk
    first = ssm_new_all is None
    phases = n_layers - layer if first else 1
    held = lambda p, i: (jnp.where(p == 0, i, steps - 1), 0, 0)
    row = pl.BlockSpec((blk, 1, D_INNER), held)
    grp = pl.BlockSpec((blk, GROUPS, STATE), held)
    h_old = pl.BlockSpec((blk, D_INNER, STATE),
                         lambda p, i: (layer * steps + jnp.where(p == 0, i, steps - 1), 0, 0))
    h_new = pl.BlockSpec((blk, D_INNER, STATE), lambda p, i: ((layer + p) * steps + i, 0, 0))
    carried = ssm_all.reshape(n_layers * n, D_INNER, STATE)
    ssm_new_all, y = pl.pallas_call(
        _ssd_sample_state_kernel,
        out_shape=(jax.ShapeDtypeStruct((n_layers * n, D_INNER, STATE), F32),
                   jax.ShapeDtypeStruct((n, 1, D_INNER), F32)),
        grid=(phases, steps),
        in_specs=[pl.BlockSpec(memory_space=pltpu.SMEM), h_old, row, grp, grp,
                  pl.BlockSpec(memory_space=pl.ANY)],
        out_specs=(h_new, row),
        input_output_aliases={} if first else {5: 0},
        compiler_params=_params(2),
        name="ssd_sample_state",
    )(dec[:, :HEADS], carried, xdt.reshape(n, 1, D_INNER), bm.reshape(n, GROUPS, STATE),
      cm.reshape(n, GROUPS, STATE), carried if first else ssm_new_all)

    x_new = pl.pallas_call(
        _ssd_sample_out_kernel,
        out_shape=jax.ShapeDtypeStruct((n, D_MODEL), F32),
        grid=(1,),
        in_specs=[_resident((n, D_MODEL)), _resident((n, D_INNER)), _resident((n, D_INNER)),
                  _resident((n, D_INNER)), _resident((1, D_INNER)), _resident((1, D_INNER)),
                  _layer_weight((D_INNER, D_MODEL), layer)],
        out_specs=full((n, D_MODEL)),
        compiler_params=_params(1),
        name="ssd_sample_out",
    )(x, y.reshape(n, D_INNER), xs, z, d_wide, norm_g, w_out)
    return x_new, ssm_new_all, conv_new.reshape(n, CONV_K - 1, CONV_DIM)


def _pad_lanes(v):
    return jnp.pad(v.reshape(1, -1), ((0, 0), (0, LANES - v.shape[-1])))


def kernel(x_prompt, x_sample, state_ssm, state_conv, norm_mix_g, norm_mlp_g, norm_final_g,
           ssd_w_in, ssd_conv_w, ssd_conv_b, ssd_dt_bias, ssd_a_log, ssd_d, ssd_norm_g,
           ssd_w_out, cm_w_in, cm_b_in, cm_v_norm_g, cm_w_s, cm_b_s, cm_w_out,
           mlp_w_up, mlp_w_down):
    batch, seq, _ = x_prompt.shape
    dec_batch, dec_seq, _ = x_sample.shape
    assert dec_seq == 1 and seq % CHUNK == 0
    pos = PAST_LEN % CM_CHUNK

    xp = x_prompt.reshape(batch * seq, D_MODEL)
    xs = x_sample.reshape(dec_batch, D_MODEL)
    row = lambda v: v.reshape(1, -1)
    expand = jnp.asarray(np.kron(np.eye(LANES, HEADS, dtype=np.float32),
                                 np.ones((1, HEAD_DIM), np.float32)), BF16)

    w_ssd_in = ssd_w_in.astype(BF16)
    w_ssd_dt = jnp.pad(ssd_w_in[:, :, SSD_IN_MAIN:].astype(BF16), ((0, 0), (0, 0), (0, LANES - HEADS)))
    w_ssd_out = ssd_w_out.astype(BF16)
    w_cm_in = cm_w_in.astype(BF16)
    w_cm_out = cm_w_out.astype(BF16)
    w_up = mlp_w_up.astype(BF16)
    w_down = mlp_w_down.astype(BF16)

    ssm_p, conv_p, conv_s, v_s = [], [], [], []
    ssm_s = None
    for i in range(DEPTH):
        j = i // 2
        g_mix = row(norm_mix_g[i])
        if i % 2 == 0:
            shared = (ssd_conv_w[j], row(ssd_conv_b[j]), _pad_lanes(ssd_dt_bias[j]),
                      _pad_lanes(ssd_a_log[j]))
            d_wide = jnp.repeat(ssd_d[j], HEAD_DIM).reshape(1, D_INNER)
            n_g = row(ssd_norm_g[j])
            xp, s_p, c_p = _ssd_prompt(xp, batch, j, g_mix, w_ssd_in, w_ssd_dt, *shared, d_wide, n_g,
                                       w_ssd_out)
            ssm_p.append(s_p.reshape(batch, HEADS, HEAD_DIM, STATE))
            conv_p.append(c_p)
            xs, ssm_s, c_s = _ssd_sample(xs, state_ssm, state_conv, ssm_s, j, g_mix, w_ssd_in, w_ssd_dt,
                                         *shared, expand, d_wide, n_g, w_ssd_out)
            conv_s.append(c_s)
        else:
            b_in = row(cm_b_in[j])
            v_g = row(cm_v_norm_g[j])
            b_s_wide = jnp.repeat(cm_b_s[j].T, CM_GROUP_DIM, axis=1)
            w_diag = jnp.repeat(cm_w_s[j][:, pos, pos], CM_GROUP_DIM).reshape(1, CM_WIDTH)
            b_diag = jnp.repeat(cm_b_s[j][:, pos], CM_GROUP_DIM).reshape(1, CM_WIDTH)
            xp, xs, v = _cm(xp, xs, j, g_mix, w_cm_in, b_in, v_g, cm_w_s, b_s_wide, w_diag, b_diag,
                            w_cm_out)
            v_s.append(v.reshape(dec_batch, dec_seq, CM_WIDTH))
        xp, xs = _mlp(xp, xs, i, row(norm_mlp_g[i]), w_up, w_down, row(norm_final_g), i == DEPTH - 1)

    return (xp.reshape(batch, seq, D_MODEL), xs.reshape(dec_batch, dec_seq, D_MODEL),
            jnp.stack(ssm_p), jnp.stack(conv_p),
            ssm_s.reshape(state_ssm.shape), jnp.stack(conv_s), jnp.stack(v_s))
```

```python
import functools

import numpy as np
import jax
import jax.numpy as jnp
from jax import lax
from jax.experimental import pallas as pl
from jax.experimental.pallas import tpu as pltpu

F32 = jnp.float32
BF16 = jnp.bfloat16

D_MODEL = 1024
DEPTH = 4
D_INNER = 2048
HEAD_DIM = 64
HEADS = 32
GROUPS = 8
HEADS_PER_GROUP = 4
STATE = 128
CONV_K = 4
CHUNK = 128
CONV_DIM = D_INNER + 2 * GROUPS * STATE
GROUP_WIDTH = D_INNER // GROUPS
CM_CHUNK = 128
CM_WIDTH = 2 * D_MODEL
CM_GROUPS = 8
CM_GROUP_DIM = CM_WIDTH // CM_GROUPS
D_FF = 4 * D_MODEL
PAST_LEN = 16384
RMS_EPS = 1e-5

LANES = 128
SUBLANES = 8
CHUNK_TILES = CHUNK // SUBLANES
WRAP_ROWS = (CONV_K - 1) * SUBLANES
CHUNK_EXT = WRAP_ROWS + CHUNK
VMEM_LIMIT = 56 * 1024 * 1024
SSD_ROWS = 512
CM_ROWS = 1024
MLP_ROWS = 1024
FF_BLOCK = 1024
COL_BLOCK = 512
SAMPLE_BLOCK = 8
CAST_ROWS = 256
SQRT_HALF = np.sqrt(0.5).astype(np.float32)
Z_COLS = slice(0, D_INNER)
XBC_COLS = slice(D_INNER, D_INNER + CONV_DIM)
SSD_IN_MAIN = D_INNER + CONV_DIM
SSD_IN_DIM = SSD_IN_MAIN + HEADS


def _rmsnorm(x, g):
    ms = jnp.mean(x * x, axis=-1, keepdims=True)
    return x * lax.rsqrt(ms + RMS_EPS) * g


def _gelu(x):
    return 0.5 * x * (1.0 + lax.erf(x * SQRT_HALF))


def _silu(x):
    h = 0.5 * x
    return h + h * jnp.tanh(h)


def _dot(a, b):
    return jnp.dot(a, b, preferred_element_type=F32)


def _resident(shape):
    nd = len(shape)
    return pl.BlockSpec(shape, lambda *_: (0,) * nd, pipeline_mode=pl.Buffered(1))


def _layer_weight(shape, layer):
    nd = len(shape)
    return pl.BlockSpec((None,) + tuple(shape), lambda *_: (layer,) + (0,) * nd,
                        pipeline_mode=pl.Buffered(1))


def _params(n_axes):
    return pltpu.CompilerParams(
        dimension_semantics=("arbitrary",) * n_axes, vmem_limit_bytes=VMEM_LIMIT)


def _mlp_kernel(xp_ref, xs_ref, g_ref, wup_ref, wdown_ref, gf_ref, op_ref, os_ref, *, final):
    def rows_through_mlp(x_ref, o_ref):
        x = x_ref[...]
        hn = _rmsnorm(x, g_ref[...]).astype(BF16)
        acc = x
        for c in range(D_FF // FF_BLOCK):
            cols = slice(c * FF_BLOCK, (c + 1) * FF_BLOCK)
            a = jnp.maximum(_dot(hn, wup_ref[:, cols]), 0.0)
            acc = acc + _dot((a * a).astype(BF16), wdown_ref[cols, :])
        if final:
            acc = _rmsnorm(acc, gf_ref[...])
        o_ref[...] = acc

    last = pl.num_programs(0) - 1
    pl.when(pl.program_id(0) < last)(functools.partial(rows_through_mlp, xp_ref, op_ref))
    pl.when(pl.program_id(0) == last)(functools.partial(rows_through_mlp, xs_ref, os_ref))


def _mlp(xp, xs, layer, g, w_up, w_down, g_final, final):
    n, ns = xp.shape[0], xs.shape[0]
    rows = min(MLP_ROWS, n)
    tiles = n // rows
    tile = pl.BlockSpec((rows, D_MODEL), lambda i: (jnp.minimum(i, tiles - 1), 0))
    return pl.pallas_call(
        functools.partial(_mlp_kernel, final=final),
        out_shape=(jax.ShapeDtypeStruct((n, D_MODEL), F32),
                   jax.ShapeDtypeStruct((ns, D_MODEL), F32)),
        grid=(tiles + 1,),
        in_specs=[tile,
                  _resident((ns, D_MODEL)),
                  _resident((1, D_MODEL)),
                  _layer_weight((D_MODEL, D_FF), layer),
                  _layer_weight((D_FF, D_MODEL), layer),
                  _resident((1, D_MODEL))],
        out_specs=(tile, pl.BlockSpec((ns, D_MODEL), lambda i: (0, 0))),
        compiler_params=_params(1),
        name="mlp",
    )(xp, xs, g, w_up, w_down, g_final)


def _cm_kernel(xp_ref, xs_ref, g_ref, win_ref, bin_ref, vg_ref, ws_ref, bs_ref, wdiag_ref, bdiag_ref,
               wout_ref, op_ref, os_ref, vs_ref, u_sc, v_sc, y_sc):
    last = pl.num_programs(0) - 1
    pl.when(pl.program_id(0) < last)(functools.partial(
        _cm_prompt_tile, xp_ref, g_ref, win_ref, bin_ref, vg_ref, ws_ref, bs_ref, wout_ref, op_ref,
        u_sc, v_sc, y_sc))
    pl.when(pl.program_id(0) == last)(functools.partial(
        _cm_sample_rows, xs_ref, g_ref, win_ref, bin_ref, vg_ref, wdiag_ref, bdiag_ref, wout_ref,
        os_ref, vs_ref))


def _cm_prompt_tile(x_ref, g_ref, win_ref, bin_ref, vg_ref, ws_ref, bs_ref, wout_ref,
                    o_ref, u_sc, v_sc, y_sc):
    rows = x_ref.shape[0]
    x = x_ref[...]
    hn = _rmsnorm(x, g_ref[...]).astype(BF16)
    for c in range(CM_WIDTH // COL_BLOCK):
        cols = slice(c * COL_BLOCK, (c + 1) * COL_BLOCK)
        u_sc[:, cols] = _gelu(_dot(hn, win_ref[:, cols]) + bin_ref[:, cols])
    ss = jnp.zeros((rows, 1), F32)
    for c in range(CM_WIDTH // COL_BLOCK):
        cols = slice(c * COL_BLOCK, (c + 1) * COL_BLOCK)
        wcols = slice(CM_WIDTH + c * COL_BLOCK, CM_WIDTH + (c + 1) * COL_BLOCK)
        gv = _gelu(_dot(hn, win_ref[:, wcols]) + bin_ref[:, wcols])
        ss = ss + jnp.sum(gv * gv, axis=-1, keepdims=True)
        v_sc[:, cols] = gv
    r = lax.rsqrt(ss * (1.0 / CM_WIDTH) + RMS_EPS)
    t_idx = lax.broadcasted_iota(jnp.int32, (CM_CHUNK, CM_CHUNK), 0)
    s_idx = lax.broadcasted_iota(jnp.int32, (CM_CHUNK, CM_CHUNK), 1)
    causal = t_idx >= s_idx
    for g in range(CM_GROUPS):
        cols = slice(g * CM_GROUP_DIM, (g + 1) * CM_GROUP_DIM)
        w = jnp.where(causal, ws_ref[g], 0.0).astype(BF16)
        for c in range(rows // CM_CHUNK):
            rs = slice(c * CM_CHUNK, (c + 1) * CM_CHUNK)
            v = (v_sc[rs, cols] * r[rs] * vg_ref[:, cols]).astype(BF16)
            s = _dot(w, v) + bs_ref[:, cols]
            y_sc[rs, cols] = (u_sc[rs, cols] * s).astype(BF16)
    o_ref[...] = x + _dot(y_sc[...], wout_ref[...])


def _cm_sample_rows(x_ref, g_ref, win_ref, bin_ref, vg_ref, wdiag_ref, bdiag_ref, wout_ref,
                    o_ref, v_ref):
    x = x_ref[...]
    hn = _rmsnorm(x, g_ref[...]).astype(BF16)
    u = _gelu(_dot(hn, win_ref[:, :CM_WIDTH]) + bin_ref[:, :CM_WIDTH])
    gv = _gelu(_dot(hn, win_ref[:, CM_WIDTH:]) + bin_ref[:, CM_WIDTH:])
    v = _rmsnorm(gv, vg_ref[...])
    v_ref[...] = v
    s = v * wdiag_ref[...] + bdiag_ref[...]
    o_ref[...] = x + _dot((u * s).astype(BF16), wout_ref[...])


def _cm(xp, xs, layer, g, w_in, b_in, v_g, w_s, b_s_wide, w_diag, b_diag, w_out):
    n, ns = xp.shape[0], xs.shape[0]
    rows = min(CM_ROWS, n)
    tiles = n // rows
    tile = pl.BlockSpec((rows, D_MODEL), lambda i: (jnp.minimum(i, tiles - 1), 0))
    whole = lambda width: pl.BlockSpec((ns, width), lambda i: (0, 0))
    return pl.pallas_call(
        _cm_kernel,
        out_shape=(jax.ShapeDtypeStruct((n, D_MODEL), F32),
                   jax.ShapeDtypeStruct((ns, D_MODEL), F32),
                   jax.ShapeDtypeStruct((ns, CM_WIDTH), F32)),
        grid=(tiles + 1,),
        in_specs=[tile,
                  _resident((ns, D_MODEL)),
                  _resident((1, D_MODEL)),
                  _layer_weight((D_MODEL, 2 * CM_WIDTH), layer),
                  _resident((1, 2 * CM_WIDTH)),
                  _resident((1, CM_WIDTH)),
                  _layer_weight((CM_GROUPS, CM_CHUNK, CM_CHUNK), layer),
                  _resident((CM_CHUNK, CM_WIDTH)),
                  _resident((1, CM_WIDTH)),
                  _resident((1, CM_WIDTH)),
                  _layer_weight((CM_WIDTH, D_MODEL), layer)],
        out_specs=(tile, whole(D_MODEL), whole(CM_WIDTH)),
        scratch_shapes=[pltpu.VMEM((rows, CM_WIDTH), F32),
                        pltpu.VMEM((rows, CM_WIDTH), F32),
                        pltpu.VMEM((rows, CM_WIDTH), BF16)],
        compiler_params=_params(1),
        name="cm",
    )(xp, xs, g, w_in, b_in, v_g, w_s, b_s_wide, w_diag, b_diag, w_out)


def _ssd_prompt_kernel(x_ref, g_ref, win_ref, wdt_ref, convw_ref, convb_ref, dtb_ref,
                       alog_ref, dskip_ref, ng_ref, wout_ref,
                       o_ref, ssm_ref, convo_ref,
                       ext_sc, carry_sc, z_sc, dt_sc, act_sc, yn_sc, ht_sc):
    rows = x_ref.shape[0]
    n_chunks = rows // CHUNK
    step = pl.program_id(1)

    @pl.when(step == 0)
    def _():
        carry_sc[...] = jnp.zeros(carry_sc.shape, F32)
        ht_sc[...] = jnp.zeros(ht_sc.shape, F32)

    def token_of(r):
        return (r % SUBLANES) * CHUNK_TILES + r // SUBLANES

    q_idx = lax.broadcasted_iota(jnp.int32, (CHUNK, CHUNK), 0)
    s_idx = lax.broadcasted_iota(jnp.int32, (CHUNK, CHUNK), 1)
    to_rows = (s_idx == token_of(q_idx)).astype(BF16)
    to_tokens = (q_idx == token_of(s_idx)).astype(BF16)
    causal = token_of(q_idx) >= token_of(s_idx)
    tril = causal.astype(F32)
    low = s_idx < HEAD_DIM

    hn = []
    for c in range(n_chunks):
        h = _rmsnorm(x_ref[c * CHUNK:(c + 1) * CHUNK, :], g_ref[...]).astype(BF16)
        hn.append(_dot(to_rows, h).astype(BF16))
    hn = jnp.concatenate(hn, axis=0)
    z_sc[...] = _dot(hn, win_ref[:, Z_COLS])
    dt_sc[...] = _dot(hn, wdt_ref[...])
    xbc = _dot(hn, win_ref[:, XBC_COLS])
    for c in range(n_chunks):
        ext_sc[c * CHUNK_EXT + WRAP_ROWS:(c + 1) * CHUNK_EXT, :] = xbc[c * CHUNK:(c + 1) * CHUNK]

    sub0 = lax.broadcasted_iota(jnp.int32, (WRAP_ROWS, COL_BLOCK), 0) % SUBLANES == 0
    for c in range(n_chunks):
        base = c * CHUNK_EXT
        for cblk in range(CONV_DIM // COL_BLOCK):
            cols = slice(cblk * COL_BLOCK, (cblk + 1) * COL_BLOCK)
            moved = jnp.concatenate(
                [pltpu.roll(ext_sc[base + CHUNK_EXT - WRAP_ROWS + m * SUBLANES:
                                   base + CHUNK_EXT - WRAP_ROWS + (m + 1) * SUBLANES, cols], 1, 0)
                 for m in range(CONV_K - 1)], axis=0)
            ext_sc[base:base + WRAP_ROWS, cols] = jnp.where(sub0, carry_sc[:, cols], moved)
            carry_sc[:, cols] = moved
            half = 0.5 * convb_ref[:, cols] + ext_sc[base:base + CHUNK, cols] * (0.5 * convw_ref[0:1, cols])
            for k in range(1, CONV_K):
                half = half + (ext_sc[base + k * SUBLANES:base + k * SUBLANES + CHUNK, cols]
                               * (0.5 * convw_ref[k:k + 1, cols]))
            act_sc[c * CHUNK:(c + 1) * CHUNK, cols] = half + half * jnp.tanh(half)
    last = n_chunks * CHUNK_EXT
    for m in range(CONV_K - 1):
        r = last - (CONV_K - 1 - m) * SUBLANES + SUBLANES - 1
        convo_ref[0, m:m + 1, :] = ext_sc[r:r + 1, :]

    a_row = -jnp.exp(alog_ref[...])
    b_off = D_INNER
    c_off = D_INNER + GROUPS * STATE

    for c in range(n_chunks):
        rs = slice(c * CHUNK, (c + 1) * CHUNK)
        dt = jax.nn.softplus(dt_sc[rs, :] + dtb_ref[...])
        a = dt * a_row
        acum = jnp.dot(tril, a, precision=lax.Precision.HIGHEST, preferred_element_type=F32)
        acum_t = acum.T
        dt_t = dt.T
        w_t = jnp.exp(acum_t[:, CHUNK - 1:CHUNK] - acum_t) * dt_t
        cdec = jnp.exp(acum[CHUNK - 1:CHUNK, :])
        arow_t = acum_t - jnp.log(dt_t)
        for g in range(GROUPS):
            bg = act_sc[rs, b_off + g * STATE:b_off + (g + 1) * STATE]
            cg = act_sc[rs, c_off + g * STATE:c_off + (g + 1) * STATE].astype(BF16)
            bt = bg.T
            cb = _dot(cg, bt.astype(BF16))
            ht = ht_sc[g]
            y_off = _dot(cg, ht.astype(BF16))
            ys = []
            for pr in range(HEADS_PER_GROUP // 2):
                h0 = g * HEADS_PER_GROUP + 2 * pr
                lanes = slice(h0 * HEAD_DIM, (h0 + 2) * HEAD_DIM)
                half = slice(pr * LANES, (pr + 1) * LANES)
                xs = act_sc[rs, lanes]
                rhs = jnp.concatenate([jnp.where(low, xs, 0.0).astype(BF16),
                                       jnp.where(low, 0.0, xs).astype(BF16)], axis=0)
                ms, bs, cols_b = [], [], []
                for hd in (h0, h0 + 1):
                    col_b = jnp.broadcast_to(acum[:, hd:hd + 1], (CHUNK, CHUNK))
                    seg = col_b - arow_t[hd:hd + 1, :]
                    ms.append((cb * jnp.exp(jnp.where(causal, seg, -jnp.inf))).astype(BF16))
                    bs.append((bt * w_t[hd:hd + 1, :]).astype(BF16))
                    cols_b.append(col_b)
                lhs = jnp.concatenate([jnp.concatenate(ms, axis=1),
                                       jnp.concatenate(bs, axis=1)], axis=0)
                res = _dot(lhs, rhs)
                e = jnp.exp(jnp.where(low, cols_b[0], cols_b[1]))
                y = res[0:CHUNK] + e * y_off[:, half] + xs * dskip_ref[:, lanes]
                cd = jnp.where(low[0:1], cdec[:, h0:h0 + 1], cdec[:, h0 + 1:h0 + 2])
                ht_sc[g, :, half] = ht[:, half] * cd + res[CHUNK:2 * CHUNK]
                ys.append(y)
            gcols = slice(g * GROUP_WIDTH, (g + 1) * GROUP_WIDTH)
            yg = jnp.concatenate(ys, axis=1)
            zg = z_sc[rs, gcols]
            yg = yg * _silu(zg)
            yg = yg * lax.rsqrt(jnp.mean(yg * yg, axis=-1, keepdims=True) + RMS_EPS)
            yn_sc[rs, gcols] = (yg * ng_ref[:, gcols]).astype(BF16)
        yn = _dot(to_tokens, yn_sc[rs, :]).astype(BF16)
        o_ref[rs, :] = x_ref[rs, :] + _dot(yn, wout_ref[...])

    @pl.when(step == pl.num_programs(1) - 1)
    def _():
        for g in range(GROUPS):
            ssm_ref[0, g * GROUP_WIDTH:(g + 1) * GROUP_WIDTH, :] = ht_sc[g].T


def _ssd_prompt(x, batch, layer, g, w_in, w_dt, conv_w, conv_b, dt_bias, a_log, d_wide, norm_g, w_out):
    n = x.shape[0]
    seq = n // batch
    rows = min(SSD_ROWS, seq)
    steps = seq // rows
    return pl.pallas_call(
        _ssd_prompt_kernel,
        out_shape=(jax.ShapeDtypeStruct((n, D_MODEL), F32),
                   jax.ShapeDtypeStruct((batch, D_INNER, STATE), F32),
                   jax.ShapeDtypeStruct((batch, CONV_K - 1, CONV_DIM), F32)),
        grid=(batch, steps),
        in_specs=[pl.BlockSpec((rows, D_MODEL), lambda b, s: (b * steps + s, 0)),
                  _resident((1, D_MODEL)),
                  _layer_weight((D_MODEL, SSD_IN_DIM), layer),
                  _layer_weight((D_MODEL, LANES), layer),
                  _resident((CONV_K, CONV_DIM)),
                  _resident((1, CONV_DIM)),
                  _resident((1, LANES)),
                  _resident((1, LANES)),
                  _resident((1, D_INNER)),
                  _resident((1, D_INNER)),
                  _layer_weight((D_INNER, D_MODEL), layer)],
        out_specs=(pl.BlockSpec((rows, D_MODEL), lambda b, s: (b * steps + s, 0)),
                   pl.BlockSpec((1, D_INNER, STATE), lambda b, s: (b, 0, 0)),
                   pl.BlockSpec((1, CONV_K - 1, CONV_DIM), lambda b, s: (b, 0, 0))),
        scratch_shapes=[pltpu.VMEM((rows // CHUNK * CHUNK_EXT, CONV_DIM), F32),
                        pltpu.VMEM((WRAP_ROWS, CONV_DIM), F32),
                        pltpu.VMEM((rows, D_INNER), F32),
                        pltpu.VMEM((rows, LANES), F32),
                        pltpu.VMEM((rows, CONV_DIM), F32),
                        pltpu.VMEM((rows, D_INNER), BF16),
                        pltpu.VMEM((GROUPS, STATE, GROUP_WIDTH), F32)],
        compiler_params=_params(2),
        name="ssd_prompt",
    )(x, g, w_in, w_dt, conv_w, conv_b, dt_bias, a_log, d_wide, norm_g, w_out)


def _ssd_sample_in_kernel(x_ref, g_ref, win_ref, wdt_ref, cs_ref, convw_ref, convb_ref,
                          dtb_ref, alog_ref, expand_ref,
                          z_ref, xs_ref, b_ref, c_ref, xdt_ref, dec_ref, convo_ref):
    hn = _rmsnorm(x_ref[...], g_ref[...]).astype(BF16)
    z_ref[...] = _dot(hn, win_ref[:, Z_COLS])
    xbc = _dot(hn, win_ref[:, XBC_COLS])
    dt_raw = _dot(hn, wdt_ref[...])
    conv = convb_ref[...] + cs_ref[:, 0:CONV_DIM] * convw_ref[0:1, :]
    for k in range(1, CONV_K - 1):
        conv = conv + cs_ref[:, k * CONV_DIM:(k + 1) * CONV_DIM] * convw_ref[k:k + 1, :]
    conv = conv + xbc * convw_ref[CONV_K - 1:CONV_K, :]
    convo_ref[:, 0:(CONV_K - 2) * CONV_DIM] = cs_ref[:, CONV_DIM:(CONV_K - 1) * CONV_DIM]
    convo_ref[:, (CONV_K - 2) * CONV_DIM:] = xbc
    act = _silu(conv)
    xs = act[:, :D_INNER]
    xs_ref[...] = xs
    b_ref[...] = act[:, D_INNER:D_INNER + GROUPS * STATE]
    c_ref[...] = act[:, D_INNER + GROUPS * STATE:]
    dt = jax.nn.softplus(dt_raw + dtb_ref[...])
    dec = jnp.exp(dt * (-jnp.exp(alog_ref[...])))

    def widen(v):
        hi = v.astype(BF16)
        lo = (v - hi.astype(F32)).astype(BF16)
        return _dot(hi, expand_ref[...]) + _dot(lo, expand_ref[...])

    xdt_ref[...] = xs * widen(dt)
    dec_ref[...] = dec


def _ssd_sample_state_kernel(dec_ref, h_ref, xdt_ref, b_ref, c_ref, stacked_ref, ho_ref, y_ref):
    del stacked_ref
    phase = pl.program_id(0)

    @pl.when(phase != 0)
    def _():
        ho_ref[...] = jnp.zeros(ho_ref.shape, F32)

    @pl.when(phase == 0)
    def _():
        base = pl.program_id(1) * SAMPLE_BLOCK
        g_idx = lax.broadcasted_iota(jnp.int32, (GROUPS, D_INNER), 0)
        l_idx = lax.broadcasted_iota(jnp.int32, (GROUPS, D_INNER), 1)
        own = (l_idx // GROUP_WIDTH) == g_idx
        for bi in range(SAMPLE_BLOCK):
            u_t = jnp.where(own, xdt_ref[bi], 0.0).astype(BF16)
            new = lax.dot_general(u_t, b_ref[bi].astype(BF16), (((0,), (0,)), ((), ())),
                                  preferred_element_type=F32)
            parts = []
            for hd in range(HEADS):
                rows = slice(hd * HEAD_DIM, (hd + 1) * HEAD_DIM)
                parts.append(h_ref[bi, rows, :] * dec_ref[base + bi, hd] + new[rows])
            h_new = jnp.concatenate(parts, axis=0)
            ho_ref[bi] = h_new
            y_all = lax.dot_general(c_ref[bi].astype(BF16), h_new.astype(BF16),
                                    (((1,), (1,)), ((), ())), preferred_element_type=F32)
            y_ref[bi] = jnp.sum(jnp.where(own, y_all, 0.0), axis=0, keepdims=True)


def _ssd_sample_out_kernel(x_ref, y_ref, xs_ref, z_ref, dskip_ref, ng_ref, wout_ref, o_ref):
    y = y_ref[...] + xs_ref[...] * dskip_ref[...]
    z = z_ref[...]
    y = y * _silu(z)
    parts = []
    for g in range(GROUPS):
        yg = y[:, g * GROUP_WIDTH:(g + 1) * GROUP_WIDTH]
        parts.append(yg * lax.rsqrt(jnp.mean(yg * yg, axis=-1, keepdims=True) + RMS_EPS))
    yn = (jnp.concatenate(parts, axis=1) * ng_ref[...]).astype(BF16)
    o_ref[...] = x_ref[...] + _dot(yn, wout_ref[...])


def _ssd_sample(x, ssm_all, conv_all, ssm_new_all, layer, g, w_in, w_dt, conv_w, conv_b,
                dt_bias, a_log, expand, d_wide, norm_g, w_out):
    n = x.shape[0]
    n_layers = ssm_all.shape[0]
    gs = GROUPS * STATE
    full = lambda shape: pl.BlockSpec(shape, lambda i: (0,) * len(shape))
    z, xs, bm, cm, xdt, dec, conv_new = pl.pallas_call(
        _ssd_sample_in_kernel,
        out_shape=(jax.ShapeDtypeStruct((n, D_INNER), F32),
                   jax.ShapeDtypeStruct((n, D_INNER), F32),
                   jax.ShapeDtypeStruct((n, gs), F32),
                   jax.ShapeDtypeStruct((n, gs), F32),
                   jax.ShapeDtypeStruct((n, D_INNER), F32),
                   jax.ShapeDtypeStruct((n, LANES), F32),
                   jax.ShapeDtypeStruct((n, (CONV_K - 1) * CONV_DIM), F32)),
        grid=(1,),
        in_specs=[_resident((n, D_MODEL)),
                  _resident((1, D_MODEL)),
                  _layer_weight((D_MODEL, SSD_IN_DIM), layer),
                  _layer_weight((D_MODEL, LANES), layer),
                  pl.BlockSpec((n, (CONV_K - 1) * CONV_DIM), lambda i: (layer, 0),
                               pipeline_mode=pl.Buffered(1)),
                  _resident((CONV_K, CONV_DIM)),
                  _resident((1, CONV_DIM)),
                  _resident((1, LANES)),
                  _resident((1, LANES)),
                  _resident((LANES, D_INNER))],
        out_specs=(full((n, D_INNER)), full((n, D_INNER)), full((n, gs)), full((n, gs)),
                   full((n, D_INNER)), full((n, LANES)), full((n, (CONV_K - 1) * CONV_DIM))),
        compiler_params=_params(1),
        name="ssd_sample_in",
    )(x, g, w_in, w_dt, conv_all.reshape(n_layers * n, (CONV_K - 1) * CONV_DIM), conv_w, conv_b,
      dt_bias, a_log, expand)

    blk = SAMPLE_BLOCK
    steps = n // blk
    first = ssm_new_all is None
    phases = n_layers - layer if first else 1
    held = lambda p, i: (jnp.where(p == 0, i, steps - 1), 0, 0)
    row = pl.BlockSpec((blk, 1, D_INNER), held)
    grp = pl.BlockSpec((blk, GROUPS, STATE), held)
    h_old = pl.BlockSpec((blk, D_INNER, STATE),
                         lambda p, i: (layer * steps + jnp.where(p == 0, i, steps - 1), 0, 0))
    h_new = pl.BlockSpec((blk, D_INNER, STATE), lambda p, i: ((layer + p) * steps + i, 0, 0))
    carried = ssm_all.reshape(n_layers * n, D_INNER, STATE)
    ssm_new_all, y = pl.pallas_call(
        _ssd_sample_state_kernel,
        out_shape=(jax.ShapeDtypeStruct((n_layers * n, D_INNER, STATE), F32),
                   jax.ShapeDtypeStruct((n, 1, D_INNER), F32)),
        grid=(phases, steps),
        in_specs=[pl.BlockSpec(memory_space=pltpu.SMEM), h_old, row, grp, grp,
                  pl.BlockSpec(memory_space=pl.ANY)],
        out_specs=(h_new, row),
        input_output_aliases={} if first else {5: 0},
        compiler_params=_params(2),
        name="ssd_sample_state",
    )(dec[:, :HEADS], carried, xdt.reshape(n, 1, D_INNER), bm.reshape(n, GROUPS, STATE),
      cm.reshape(n, GROUPS, STATE), carried if first else ssm_new_all)

    x_new = pl.pallas_call(
        _ssd_sample_out_kernel,
        out_shape=jax.ShapeDtypeStruct((n, D_MODEL), F32),
        grid=(1,),
        in_specs=[_resident((n, D_MODEL)), _resident((n, D_INNER)), _resident((n, D_INNER)),
                  _resident((n, D_INNER)), _resident((1, D_INNER)), _resident((1, D_INNER)),
                  _layer_weight((D_INNER, D_MODEL), layer)],
        out_specs=full((n, D_MODEL)),
        compiler_params=_params(1),
        name="ssd_sample_out",
    )(x, y.reshape(n, D_INNER), xs, z, d_wide, norm_g, w_out)
    return x_new, ssm_new_all, conv_new.reshape(n, CONV_K - 1, CONV_DIM)


def _to_bf16_kernel(w_ref, o_ref):
    o_ref[...] = w_ref[...].astype(BF16)


def _to_bf16(w):
    layers, rows, cols = w.shape
    blk = pl.BlockSpec((None, CAST_ROWS, cols), lambda l, i: (l, i, 0))
    return pl.pallas_call(
        _to_bf16_kernel,
        out_shape=jax.ShapeDtypeStruct(w.shape, BF16),
        grid=(layers, rows // CAST_ROWS),
        in_specs=[blk],
        out_specs=blk,
        compiler_params=_params(2),
        name="to_bf16",
    )(w)


def _pad_lanes(v):
    return jnp.pad(v.reshape(1, -1), ((0, 0), (0, LANES - v.shape[-1])))


def kernel(x_prompt, x_sample, state_ssm, state_conv, norm_mix_g, norm_mlp_g, norm_final_g,
           ssd_w_in, ssd_conv_w, ssd_conv_b, ssd_dt_bias, ssd_a_log, ssd_d, ssd_norm_g,
           ssd_w_out, cm_w_in, cm_b_in, cm_v_norm_g, cm_w_s, cm_b_s, cm_w_out,
           mlp_w_up, mlp_w_down):
    batch, seq, _ = x_prompt.shape
    dec_batch, dec_seq, _ = x_sample.shape
    assert dec_seq == 1 and seq % CHUNK == 0
    pos = PAST_LEN % CM_CHUNK

    xp = x_prompt.reshape(batch * seq, D_MODEL)
    xs = x_sample.reshape(dec_batch, D_MODEL)
    row = lambda v: v.reshape(1, -1)
    expand = jnp.asarray(np.kron(np.eye(LANES, HEADS, dtype=np.float32),
                                 np.ones((1, HEAD_DIM), np.float32)), BF16)

    w_ssd_in = _to_bf16(ssd_w_in)
    w_ssd_dt = jnp.pad(ssd_w_in[:, :, SSD_IN_MAIN:].astype(BF16), ((0, 0), (0, 0), (0, LANES - HEADS)))
    w_ssd_out = ssd_w_out.astype(BF16)
    w_cm_in = cm_w_in.astype(BF16)
    w_cm_out = cm_w_out.astype(BF16)
    w_up = mlp_w_up.astype(BF16)
    w_down = mlp_w_down.astype(BF16)

    ssm_p, conv_p, conv_s, v_s = [], [], [], []
    ssm_s = None
    for i in range(DEPTH):
        j = i // 2
        g_mix = row(norm_mix_g[i])
        if i % 2 == 0:
            shared = (ssd_conv_w[j], row(ssd_conv_b[j]), _pad_lanes(ssd_dt_bias[j]),
                      _pad_lanes(ssd_a_log[j]))
            d_wide = jnp.repeat(ssd_d[j], HEAD_DIM).reshape(1, D_INNER)
            n_g = row(ssd_norm_g[j])
            xp, s_p, c_p = _ssd_prompt(xp, batch, j, g_mix, w_ssd_in, w_ssd_dt, *shared, d_wide, n_g,
                                       w_ssd_out)
            ssm_p.append(s_p.reshape(batch, HEADS, HEAD_DIM, STATE))
            conv_p.append(c_p)
            xs, ssm_s, c_s = _ssd_sample(xs, state_ssm, state_conv, ssm_s, j, g_mix, w_ssd_in, w_ssd_dt,
                                         *shared, expand, d_wide, n_g, w_ssd_out)
            conv_s.append(c_s)
        else:
            b_in = row(cm_b_in[j])
            v_g = row(cm_v_norm_g[j])
            b_s_wide = jnp.repeat(cm_b_s[j].T, CM_GROUP_DIM, axis=1)
            w_diag = jnp.repeat(cm_w_s[j][:, pos, pos], CM_GROUP_DIM).reshape(1, CM_WIDTH)
            b_diag = jnp.repeat(cm_b_s[j][:, pos], CM_GROUP_DIM).reshape(1, CM_WIDTH)
            xp, xs, v = _cm(xp, xs, j, g_mix, w_cm_in, b_in, v_g, cm_w_s, b_s_wide, w_diag, b_diag,
                            w_cm_out)
            v_s.append(v.reshape(dec_batch, dec_seq, CM_WIDTH))
        xp, xs = _mlp(xp, xs, i, row(norm_mlp_g[i]), w_up, w_down, row(norm_final_g), i == DEPTH - 1)

    return (xp.reshape(batch, seq, D_MODEL), xs.reshape(dec_batch, dec_seq, D_MODEL),
            jnp.stack(ssm_p), jnp.stack(conv_p),
            ssm_s.reshape(state_ssm.shape), jnp.stack(conv_s), jnp.stack(v_s))
```

```python
import functools

import numpy as np
import jax
import jax.numpy as jnp
from jax import lax
from jax.experimental import pallas as pl
from jax.experimental.pallas import tpu as pltpu

F32 = jnp.float32
BF16 = jnp.bfloat16

D_MODEL = 1024
DEPTH = 4
D_INNER = 2048
HEAD_DIM = 64
HEADS = 32
GROUPS = 8
HEADS_PER_GROUP = 4
STATE = 128
CONV_K = 4
CHUNK = 128
CONV_DIM = D_INNER + 2 * GROUPS * STATE
GROUP_WIDTH = D_INNER // GROUPS
CM_CHUNK = 128
CM_WIDTH = 2 * D_MODEL
CM_GROUPS = 8
CM_GROUP_DIM = CM_WIDTH // CM_GROUPS
D_FF = 4 * D_MODEL
PAST_LEN = 16384
RMS_EPS = 1e-5

LANES = 128
SUBLANES = 8
CHUNK_TILES = CHUNK // SUBLANES
WRAP_ROWS = (CONV_K - 1) * SUBLANES
CHUNK_EXT = WRAP_ROWS + CHUNK
VMEM_LIMIT = 56 * 1024 * 1024
SSD_ROWS = 512
CM_ROWS = 1024
MLP_ROWS = 1024
FF_BLOCK = 1024
COL_BLOCK = 512
OUT_ROWS = 256
SAMPLE_BLOCK = 8
SQRT_HALF = np.sqrt(0.5).astype(np.float32)
Z_COLS = slice(0, D_INNER)
XBC_COLS = slice(D_INNER, D_INNER + CONV_DIM)
SSD_IN_MAIN = D_INNER + CONV_DIM
SSD_IN_DIM = SSD_IN_MAIN + HEADS


def _rmsnorm(x, g):
    ms = jnp.mean(x * x, axis=-1, keepdims=True)
    return x * lax.rsqrt(ms + RMS_EPS) * g


def _gelu(x):
    return 0.5 * x * (1.0 + lax.erf(x * SQRT_HALF))


def _silu(x):
    h = 0.5 * x
    return h + h * jnp.tanh(h)


def _dot(a, b):
    return jnp.dot(a, b, preferred_element_type=F32)


def _resident(shape):
    nd = len(shape)
    return pl.BlockSpec(shape, lambda *_: (0,) * nd, pipeline_mode=pl.Buffered(1))


def _layer_weight(shape, layer):
    nd = len(shape)
    return pl.BlockSpec((None,) + tuple(shape), lambda *_: (layer,) + (0,) * nd,
                        pipeline_mode=pl.Buffered(1))


def _params(n_axes):
    return pltpu.CompilerParams(
        dimension_semantics=("arbitrary",) * n_axes, vmem_limit_bytes=VMEM_LIMIT)


def _mlp_kernel(xp_ref, xs_ref, g_ref, wup_ref, wdown_ref, gf_ref, op_ref, os_ref, *, final):
    def rows_through_mlp(x_ref, o_ref):
        x = x_ref[...]
        hn = _rmsnorm(x, g_ref[...]).astype(BF16)
        acc = x
        for c in range(D_FF // FF_BLOCK):
            cols = slice(c * FF_BLOCK, (c + 1) * FF_BLOCK)
            a = jnp.maximum(_dot(hn, wup_ref[:, cols]), 0.0)
            acc = acc + _dot((a * a).astype(BF16), wdown_ref[cols, :])
        if final:
            acc = _rmsnorm(acc, gf_ref[...])
        o_ref[...] = acc

    last = pl.num_programs(0) - 1
    pl.when(pl.program_id(0) < last)(functools.partial(rows_through_mlp, xp_ref, op_ref))
    pl.when(pl.program_id(0) == last)(functools.partial(rows_through_mlp, xs_ref, os_ref))


def _mlp(xp, xs, layer, g, w_up, w_down, g_final, final):
    n, ns = xp.shape[0], xs.shape[0]
    rows = min(MLP_ROWS, n)
    tiles = n // rows
    tile = pl.BlockSpec((rows, D_MODEL), lambda i: (jnp.minimum(i, tiles - 1), 0))
    return pl.pallas_call(
        functools.partial(_mlp_kernel, final=final),
        out_shape=(jax.ShapeDtypeStruct((n, D_MODEL), F32),
                   jax.ShapeDtypeStruct((ns, D_MODEL), F32)),
        grid=(tiles + 1,),
        in_specs=[tile,
                  _resident((ns, D_MODEL)),
                  _resident((1, D_MODEL)),
                  _layer_weight((D_MODEL, D_FF), layer),
                  _layer_weight((D_FF, D_MODEL), layer),
                  _resident((1, D_MODEL))],
        out_specs=(tile, pl.BlockSpec((ns, D_MODEL), lambda i: (0, 0))),
        compiler_params=_params(1),
        name="mlp",
    )(xp, xs, g, w_up, w_down, g_final)


def _cm_kernel(xp_ref, xs_ref, g_ref, win_ref, bin_ref, vg_ref, ws_ref, bs_ref, wdiag_ref, bdiag_ref,
               wout_ref, op_ref, os_ref, vs_ref, u_sc, v_sc, y_sc):
    last = pl.num_programs(0) - 1
    pl.when(pl.program_id(0) < last)(functools.partial(
        _cm_prompt_tile, xp_ref, g_ref, win_ref, bin_ref, vg_ref, ws_ref, bs_ref, wout_ref, op_ref,
        u_sc, v_sc, y_sc))
    pl.when(pl.program_id(0) == last)(functools.partial(
        _cm_sample_rows, xs_ref, g_ref, win_ref, bin_ref, vg_ref, wdiag_ref, bdiag_ref, wout_ref,
        os_ref, vs_ref))


def _cm_prompt_tile(x_ref, g_ref, win_ref, bin_ref, vg_ref, ws_ref, bs_ref, wout_ref,
                    o_ref, u_sc, v_sc, y_sc):
    rows = x_ref.shape[0]
    x = x_ref[...]
    hn = _rmsnorm(x, g_ref[...]).astype(BF16)
    for c in range(CM_WIDTH // COL_BLOCK):
        cols = slice(c * COL_BLOCK, (c + 1) * COL_BLOCK)
        u_sc[:, cols] = _gelu(_dot(hn, win_ref[:, cols]) + bin_ref[:, cols])
    ss = jnp.zeros((rows, 1), F32)
    for c in range(CM_WIDTH // COL_BLOCK):
        cols = slice(c * COL_BLOCK, (c + 1) * COL_BLOCK)
        wcols = slice(CM_WIDTH + c * COL_BLOCK, CM_WIDTH + (c + 1) * COL_BLOCK)
        gv = _gelu(_dot(hn, win_ref[:, wcols]) + bin_ref[:, wcols])
        ss = ss + jnp.sum(gv * gv, axis=-1, keepdims=True)
        v_sc[:, cols] = gv
    r = lax.rsqrt(ss * (1.0 / CM_WIDTH) + RMS_EPS)
    t_idx = lax.broadcasted_iota(jnp.int32, (CM_CHUNK, CM_CHUNK), 0)
    s_idx = lax.broadcasted_iota(jnp.int32, (CM_CHUNK, CM_CHUNK), 1)
    causal = t_idx >= s_idx
    blk = min(rows, OUT_ROWS)
    for b in range(rows // blk):
        for g in range(CM_GROUPS):
            cols = slice(g * CM_GROUP_DIM, (g + 1) * CM_GROUP_DIM)
            w = jnp.where(causal, ws_ref[g], 0.0).astype(BF16)
            for c in range(b * blk // CM_CHUNK, (b + 1) * blk // CM_CHUNK):
                rs = slice(c * CM_CHUNK, (c + 1) * CM_CHUNK)
                v = (v_sc[rs, cols] * r[rs] * vg_ref[:, cols]).astype(BF16)
                s = _dot(w, v) + bs_ref[:, cols]
                y_sc[rs, cols] = (u_sc[rs, cols] * s).astype(BF16)
        rb = slice(b * blk, (b + 1) * blk)
        o_ref[rb, :] = x_ref[rb, :] + _dot(y_sc[rb, :], wout_ref[...])


def _cm_sample_rows(x_ref, g_ref, win_ref, bin_ref, vg_ref, wdiag_ref, bdiag_ref, wout_ref,
                    o_ref, v_ref):
    x = x_ref[...]
    hn = _rmsnorm(x, g_ref[...]).astype(BF16)
    u = _gelu(_dot(hn, win_ref[:, :CM_WIDTH]) + bin_ref[:, :CM_WIDTH])
    gv = _gelu(_dot(hn, win_ref[:, CM_WIDTH:]) + bin_ref[:, CM_WIDTH:])
    v = _rmsnorm(gv, vg_ref[...])
    v_ref[...] = v
    s = v * wdiag_ref[...] + bdiag_ref[...]
    o_ref[...] = x + _dot((u * s).astype(BF16), wout_ref[...])


def _cm(xp, xs, layer, g, w_in, b_in, v_g, w_s, b_s_wide, w_diag, b_diag, w_out):
    n, ns = xp.shape[0], xs.shape[0]
    rows = min(CM_ROWS, n)
    tiles = n // rows
    tile = pl.BlockSpec((rows, D_MODEL), lambda i: (jnp.minimum(i, tiles - 1), 0))
    whole = lambda width: pl.BlockSpec((ns, width), lambda i: (0, 0))
    return pl.pallas_call(
        _cm_kernel,
        out_shape=(jax.ShapeDtypeStruct((n, D_MODEL), F32),
                   jax.ShapeDtypeStruct((ns, D_MODEL), F32),
                   jax.ShapeDtypeStruct((ns, CM_WIDTH), F32)),
        grid=(tiles + 1,),
        in_specs=[tile,
                  _resident((ns, D_MODEL)),
                  _resident((1, D_MODEL)),
                  _layer_weight((D_MODEL, 2 * CM_WIDTH), layer),
                  _resident((1, 2 * CM_WIDTH)),
                  _resident((1, CM_WIDTH)),
                  _layer_weight((CM_GROUPS, CM_CHUNK, CM_CHUNK), layer),
                  _resident((CM_CHUNK, CM_WIDTH)),
                  _resident((1, CM_WIDTH)),
                  _resident((1, CM_WIDTH)),
                  _layer_weight((CM_WIDTH, D_MODEL), layer)],
        out_specs=(tile, whole(D_MODEL), whole(CM_WIDTH)),
        scratch_shapes=[pltpu.VMEM((rows, CM_WIDTH), F32),
                        pltpu.VMEM((rows, CM_WIDTH), F32),
                        pltpu.VMEM((rows, CM_WIDTH), BF16)],
        compiler_params=_params(1),
        name="cm",
    )(xp, xs, g, w_in, b_in, v_g, w_s, b_s_wide, w_diag, b_diag, w_out)


def _ssd_prompt_kernel(x_ref, g_ref, win_ref, wdt_ref, convw_ref, convb_ref, dtb_ref,
                       alog_ref, dskip_ref, ng_ref, wout_ref,
                       o_ref, ssm_ref, convo_ref,
                       ext_sc, carry_sc, z_sc, dt_sc, act_sc, yn_sc, ht_sc):
    rows = x_ref.shape[0]
    n_chunks = rows // CHUNK
    step = pl.program_id(1)

    @pl.when(step == 0)
    def _():
        carry_sc[...] = jnp.zeros(carry_sc.shape, F32)
        ht_sc[...] = jnp.zeros(ht_sc.shape, F32)

    def token_of(r):
        return (r % SUBLANES) * CHUNK_TILES + r // SUBLANES

    q_idx = lax.broadcasted_iota(jnp.int32, (CHUNK, CHUNK), 0)
    s_idx = lax.broadcasted_iota(jnp.int32, (CHUNK, CHUNK), 1)
    to_rows = (s_idx == token_of(q_idx)).astype(BF16)
    to_tokens = (q_idx == token_of(s_idx)).astype(BF16)
    causal = token_of(q_idx) >= token_of(s_idx)
    tril = causal.astype(F32)
    low = s_idx < HEAD_DIM

    hn = []
    for c in range(n_chunks):
        h = _rmsnorm(x_ref[c * CHUNK:(c + 1) * CHUNK, :], g_ref[...]).astype(BF16)
        hn.append(_dot(to_rows, h).astype(BF16))
    hn = jnp.concatenate(hn, axis=0)
    z_sc[...] = _dot(hn, win_ref[:, Z_COLS])
    dt_sc[...] = _dot(hn, wdt_ref[...])
    xbc = _dot(hn, win_ref[:, XBC_COLS])
    for c in range(n_chunks):
        ext_sc[c * CHUNK_EXT + WRAP_ROWS:(c + 1) * CHUNK_EXT, :] = xbc[c * CHUNK:(c + 1) * CHUNK]

    sub0 = lax.broadcasted_iota(jnp.int32, (WRAP_ROWS, COL_BLOCK), 0) % SUBLANES == 0
    for c in range(n_chunks):
        base = c * CHUNK_EXT
        for cblk in range(CONV_DIM // COL_BLOCK):
            cols = slice(cblk * COL_BLOCK, (cblk + 1) * COL_BLOCK)
            moved = jnp.concatenate(
                [pltpu.roll(ext_sc[base + CHUNK_EXT - WRAP_ROWS + m * SUBLANES:
                                   base + CHUNK_EXT - WRAP_ROWS + (m + 1) * SUBLANES, cols], 1, 0)
                 for m in range(CONV_K - 1)], axis=0)
            ext_sc[base:base + WRAP_ROWS, cols] = jnp.where(sub0, carry_sc[:, cols], moved)
            carry_sc[:, cols] = moved
            conv = convb_ref[:, cols] + ext_sc[base:base + CHUNK, cols] * convw_ref[0:1, cols]
            for k in range(1, CONV_K):
                conv = conv + (ext_sc[base + k * SUBLANES:base + k * SUBLANES + CHUNK, cols]
                               * convw_ref[k:k + 1, cols])
            act_sc[c * CHUNK:(c + 1) * CHUNK, cols] = _silu(conv)
    last = n_chunks * CHUNK_EXT
    for m in range(CONV_K - 1):
        r = last - (CONV_K - 1 - m) * SUBLANES + SUBLANES - 1
        convo_ref[0, m:m + 1, :] = ext_sc[r:r + 1, :]

    a_row = -jnp.exp(alog_ref[...])
    b_off = D_INNER
    c_off = D_INNER + GROUPS * STATE

    for c in range(n_chunks):
        rs = slice(c * CHUNK, (c + 1) * CHUNK)
        dt = jax.nn.softplus(dt_sc[rs, :] + dtb_ref[...])
        a = dt * a_row
        acum = jnp.dot(tril, a, precision=lax.Precision.HIGHEST, preferred_element_type=F32)
        acum_t = acum.T
        dt_t = dt.T
        w_t = jnp.exp(acum_t[:, CHUNK - 1:CHUNK] - acum_t) * dt_t
        cdec = jnp.exp(acum[CHUNK - 1:CHUNK, :])
        arow_t = acum_t - jnp.log(dt_t)
        for g in range(GROUPS):
            bg = act_sc[rs, b_off + g * STATE:b_off + (g + 1) * STATE]
            cg = act_sc[rs, c_off + g * STATE:c_off + (g + 1) * STATE].astype(BF16)
            bt = bg.T
            cb = _dot(cg, bt.astype(BF16))
            ht = ht_sc[g]
            y_off = _dot(cg, ht.astype(BF16))
            ys = []
            for pr in range(HEADS_PER_GROUP // 2):
                h0 = g * HEADS_PER_GROUP + 2 * pr
                lanes = slice(h0 * HEAD_DIM, (h0 + 2) * HEAD_DIM)
                half = slice(pr * LANES, (pr + 1) * LANES)
                xs = act_sc[rs, lanes]
                rhs = jnp.concatenate([jnp.where(low, xs, 0.0).astype(BF16),
                                       jnp.where(low, 0.0, xs).astype(BF16)], axis=0)
                ms, bs, cols_b = [], [], []
                for hd in (h0, h0 + 1):
                    col_b = jnp.broadcast_to(acum[:, hd:hd + 1], (CHUNK, CHUNK))
                    seg = col_b - arow_t[hd:hd + 1, :]
                    ms.append((cb * jnp.exp(jnp.where(causal, seg, -jnp.inf))).astype(BF16))
                    bs.append((bt * w_t[hd:hd + 1, :]).astype(BF16))
                    cols_b.append(col_b)
                lhs = jnp.concatenate([jnp.concatenate(ms, axis=1),
                                       jnp.concatenate(bs, axis=1)], axis=0)
                res = _dot(lhs, rhs)
                e = jnp.exp(jnp.where(low, cols_b[0], cols_b[1]))
                y = res[0:CHUNK] + e * y_off[:, half] + xs * dskip_ref[:, lanes]
                cd = jnp.where(low[0:1], cdec[:, h0:h0 + 1], cdec[:, h0 + 1:h0 + 2])
                ht_sc[g, :, half] = ht[:, half] * cd + res[CHUNK:2 * CHUNK]
                ys.append(y)
            gcols = slice(g * GROUP_WIDTH, (g + 1) * GROUP_WIDTH)
            yg = jnp.concatenate(ys, axis=1)
            zg = z_sc[rs, gcols]
            yg = yg * _silu(zg)
            yg = yg * lax.rsqrt(jnp.mean(yg * yg, axis=-1, keepdims=True) + RMS_EPS)
            yn_sc[rs, gcols] = (yg * ng_ref[:, gcols]).astype(BF16)
        yn = _dot(to_tokens, yn_sc[rs, :]).astype(BF16)
        o_ref[rs, :] = x_ref[rs, :] + _dot(yn, wout_ref[...])

    @pl.when(step == pl.num_programs(1) - 1)
    def _():
        for g in range(GROUPS):
            ssm_ref[0, g * GROUP_WIDTH:(g + 1) * GROUP_WIDTH, :] = ht_sc[g].T


def _ssd_prompt(x, batch, layer, g, w_in, w_dt, conv_w, conv_b, dt_bias, a_log, d_wide, norm_g, w_out):
    n = x.shape[0]
    seq = n // batch
    rows = min(SSD_ROWS, seq)
    steps = seq // rows
    return pl.pallas_call(
        _ssd_prompt_kernel,
        out_shape=(jax.ShapeDtypeStruct((n, D_MODEL), F32),
                   jax.ShapeDtypeStruct((batch, D_INNER, STATE), F32),
                   jax.ShapeDtypeStruct((batch, CONV_K - 1, CONV_DIM), F32)),
        grid=(batch, steps),
        in_specs=[pl.BlockSpec((rows, D_MODEL), lambda b, s: (b * steps + s, 0)),
                  _resident((1, D_MODEL)),
                  _layer_weight((D_MODEL, SSD_IN_DIM), layer),
                  _layer_weight((D_MODEL, LANES), layer),
                  _resident((CONV_K, CONV_DIM)),
                  _resident((1, CONV_DIM)),
                  _resident((1, LANES)),
                  _resident((1, LANES)),
                  _resident((1, D_INNER)),
                  _resident((1, D_INNER)),
                  _layer_weight((D_INNER, D_MODEL), layer)],
        out_specs=(pl.BlockSpec((rows, D_MODEL), lambda b, s: (b * steps + s, 0)),
                   pl.BlockSpec((1, D_INNER, STATE), lambda b, s: (b, 0, 0)),
                   pl.BlockSpec((1, CONV_K - 1, CONV_DIM), lambda b, s: (b, 0, 0))),
        scratch_shapes=[pltpu.VMEM((rows // CHUNK * CHUNK_EXT, CONV_DIM), F32),
                        pltpu.VMEM((WRAP_ROWS, CONV_DIM), F32),
                        pltpu.VMEM((rows, D_INNER), F32),
                        pltpu.VMEM((rows, LANES), F32),
                        pltpu.VMEM((rows, CONV_DIM), F32),
                        pltpu.VMEM((rows, D_INNER), BF16),
                        pltpu.VMEM((GROUPS, STATE, GROUP_WIDTH), F32)],
        compiler_params=_params(2),
        name="ssd_prompt",
    )(x, g, w_in, w_dt, conv_w, conv_b, dt_bias, a_log, d_wide, norm_g, w_out)


def _ssd_sample_in_kernel(x_ref, g_ref, win_ref, wdt_ref, cs_ref, convw_ref, convb_ref,
                          dtb_ref, alog_ref, expand_ref,
                          z_ref, xs_ref, b_ref, c_ref, xdt_ref, dec_ref, convo_ref):
    hn = _rmsnorm(x_ref[...], g_ref[...]).astype(BF16)
    z_ref[...] = _dot(hn, win_ref[:, Z_COLS])
    xbc = _dot(hn, win_ref[:, XBC_COLS])
    dt_raw = _dot(hn, wdt_ref[...])
    conv = convb_ref[...] + cs_ref[:, 0:CONV_DIM] * convw_ref[0:1, :]
    for k in range(1, CONV_K - 1):
        conv = conv + cs_ref[:, k * CONV_DIM:(k + 1) * CONV_DIM] * convw_ref[k:k + 1, :]
    conv = conv + xbc * convw_ref[CONV_K - 1:CONV_K, :]
    convo_ref[:, 0:(CONV_K - 2) * CONV_DIM] = cs_ref[:, CONV_DIM:(CONV_K - 1) * CONV_DIM]
    convo_ref[:, (CONV_K - 2) * CONV_DIM:] = xbc
    act = _silu(conv)
    xs = act[:, :D_INNER]
    xs_ref[...] = xs
    b_ref[...] = act[:, D_INNER:D_INNER + GROUPS * STATE]
    c_ref[...] = act[:, D_INNER + GROUPS * STATE:]
    dt = jax.nn.softplus(dt_raw + dtb_ref[...])
    dec = jnp.exp(dt * (-jnp.exp(alog_ref[...])))

    def widen(v):
        hi = v.astype(BF16)
        lo = (v - hi.astype(F32)).astype(BF16)
        return _dot(hi, expand_ref[...]) + _dot(lo, expand_ref[...])

    xdt_ref[...] = xs * widen(dt)
    dec_ref[...] = dec


def _ssd_sample_state_kernel(dec_ref, h_ref, xdt_ref, b_ref, c_ref, stacked_ref, ho_ref, y_ref):
    del stacked_ref
    phase = pl.program_id(0)

    @pl.when(phase != 0)
    def _():
        ho_ref[...] = jnp.zeros(ho_ref.shape, F32)

    @pl.when(phase == 0)
    def _():
        base = pl.program_id(1) * SAMPLE_BLOCK
        g_idx = lax.broadcasted_iota(jnp.int32, (GROUPS, D_INNER), 0)
        l_idx = lax.broadcasted_iota(jnp.int32, (GROUPS, D_INNER), 1)
        own = (l_idx // GROUP_WIDTH) == g_idx
        for bi in range(SAMPLE_BLOCK):
            u_t = jnp.where(own, xdt_ref[bi], 0.0).astype(BF16)
            new = lax.dot_general(u_t, b_ref[bi].astype(BF16), (((0,), (0,)), ((), ())),
                                  preferred_element_type=F32)
            parts = []
            for hd in range(HEADS):
                rows = slice(hd * HEAD_DIM, (hd + 1) * HEAD_DIM)
                parts.append(h_ref[bi, rows, :] * dec_ref[base + bi, hd] + new[rows])
            h_new = jnp.concatenate(parts, axis=0)
            ho_ref[bi] = h_new
            y_all = lax.dot_general(c_ref[bi].astype(BF16), h_new.astype(BF16),
                                    (((1,), (1,)), ((), ())), preferred_element_type=F32)
            y_ref[bi] = jnp.sum(jnp.where(own, y_all, 0.0), axis=0, keepdims=True)


def _ssd_sample_out_kernel(x_ref, y_ref, xs_ref, z_ref, dskip_ref, ng_ref, wout_ref, o_ref):
    y = y_ref[...] + xs_ref[...] * dskip_ref[...]
    z = z_ref[...]
    y = y * _silu(z)
    parts = []
    for g in range(GROUPS):
        yg = y[:, g * GROUP_WIDTH:(g + 1) * GROUP_WIDTH]
        parts.append(yg * lax.rsqrt(jnp.mean(yg * yg, axis=-1, keepdims=True) + RMS_EPS))
    yn = (jnp.concatenate(parts, axis=1) * ng_ref[...]).astype(BF16)
    o_ref[...] = x_ref[...] + _dot(yn, wout_ref[...])


def _ssd_sample(x, ssm_all, conv_all, ssm_new_all, layer, g, w_in, w_dt, conv_w, conv_b,
                dt_bias, a_log, expand, d_wide, norm_g, w_out):
    n = x.shape[0]
    n_layers = ssm_all.shape[0]
    gs = GROUPS * STATE
    full = lambda shape: pl.BlockSpec(shape, lambda i: (0,) * len(shape))
    z, xs, bm, cm, xdt, dec, conv_new = pl.pallas_call(
        _ssd_sample_in_kernel,
        out_shape=(jax.ShapeDtypeStruct((n, D_INNER), F32),
                   jax.ShapeDtypeStruct((n, D_INNER), F32),
                   jax.ShapeDtypeStruct((n, gs), F32),
                   jax.ShapeDtypeStruct((n, gs), F32),
                   jax.ShapeDtypeStruct((n, D_INNER), F32),
                   jax.ShapeDtypeStruct((n, LANES), F32),
                   jax.ShapeDtypeStruct((n, (CONV_K - 1) * CONV_DIM), F32)),
        grid=(1,),
        in_specs=[_resident((n, D_MODEL)),
                  _resident((1, D_MODEL)),
                  _layer_weight((D_MODEL, SSD_IN_DIM), layer),
                  _layer_weight((D_MODEL, LANES), layer),
                  pl.BlockSpec((n, (CONV_K - 1) * CONV_DIM), lambda i: (layer, 0),
                               pipeline_mode=pl.Buffered(1)),
                  _resident((CONV_K, CONV_DIM)),
                  _resident((1, CONV_DIM)),
                  _resident((1, LANES)),
                  _resident((1, LANES)),
                  _resident((LANES, D_INNER))],
        out_specs=(full((n, D_INNER)), full((n, D_INNER)), full((n, gs)), full((n, gs)),
                   full((n, D_INNER)), full((n, LANES)), full((n, (CONV_K - 1) * CONV_DIM))),
        compiler_params=_params(1),
        name="ssd_sample_in",
    )(x, g, w_in, w_dt, conv_all.reshape(n_layers * n, (CONV_K - 1) * CONV_DIM), conv_w, conv_b,
      dt_bias, a_log, expand)

    blk = SAMPLE_BLOCK
    steps = n // blk
    first = ssm_new_all is None
    phases = n_layers - layer if first else 1
    held = lambda p, i: (jnp.where(p == 0, i, steps - 1), 0, 0)
    row = pl.BlockSpec((blk, 1, D_INNER), held)
    grp = pl.BlockSpec((blk, GROUPS, STATE), held)
    h_old = pl.BlockSpec((blk, D_INNER, STATE),
                         lambda p, i: (layer * steps + jnp.where(p == 0, i, steps - 1), 0, 0))
    h_new = pl.BlockSpec((blk, D_INNER, STATE), lambda p, i: ((layer + p) * steps + i, 0, 0))
    carried = ssm_all.reshape(n_layers * n, D_INNER, STATE)
    ssm_new_all, y = pl.pallas_call(
        _ssd_sample_state_kernel,
        out_shape=(jax.ShapeDtypeStruct((n_layers * n, D_INNER, STATE), F32),
                   jax.ShapeDtypeStruct((n, 1, D_INNER), F32)),
        grid=(phases, steps),
        in_specs=[pl.BlockSpec(memory_space=pltpu.SMEM), h_old, row, grp, grp,
                  pl.BlockSpec(memory_space=pl.ANY)],
        out_specs=(h_new, row),
        input_output_aliases={} if first else {5: 0},
        compiler_params=_params(2),
        name="ssd_sample_state",
    )(dec[:, :HEADS], carried, xdt.reshape(n, 1, D_INNER), bm.reshape(n, GROUPS, STATE),
      cm.reshape(n, GROUPS, STATE), carried if first else ssm_new_all)

    x_new = pl.pallas_call(
        _ssd_sample_out_kernel,
        out_shape=jax.ShapeDtypeStruct((n, D_MODEL), F32),
        grid=(1,),
        in_specs=[_resident((n, D_MODEL)), _resident((n, D_INNER)), _resident((n, D_INNER)),
                  _resident((n, D_INNER)), _resident((1, D_INNER)), _resident((1, D_INNER)),
                  _layer_weight((D_INNER, D_MODEL), layer)],
        out_specs=full((n, D_MODEL)),
        compiler_params=_params(1),
        name="ssd_sample_out",
    )(x, y.reshape(n, D_INNER), xs, z, d_wide, norm_g, w_out)
    return x_new, ssm_new_all, conv_new.reshape(n, CONV_K - 1, CONV_DIM)


def _pad_lanes(v):
    return jnp.pad(v.reshape(1, -1), ((0, 0), (0, LANES - v.shape[-1])))


def kernel(x_prompt, x_sample, state_ssm, state_conv, norm_mix_g, norm_mlp_g, norm_final_g,
           ssd_w_in, ssd_conv_w, ssd_conv_b, ssd_dt_bias, ssd_a_log, ssd_d, ssd_norm_g,
           ssd_w_out, cm_w_in, cm_b_in, cm_v_norm_g, cm_w_s, cm_b_s, cm_w_out,
           mlp_w_up, mlp_w_down):
    batch, seq, _ = x_prompt.shape
    dec_batch, dec_seq, _ = x_sample.shape
    assert dec_seq == 1 and seq % CHUNK == 0
    pos = PAST_LEN % CM_CHUNK

    xp = x_prompt.reshape(batch * seq, D_MODEL)
    xs = x_sample.reshape(dec_batch, D_MODEL)
    row = lambda v: v.reshape(1, -1)
    expand = jnp.asarray(np.kron(np.eye(LANES, HEADS, dtype=np.float32),
                                 np.ones((1, HEAD_DIM), np.float32)), BF16)

    w_ssd_in = ssd_w_in.astype(BF16)
    w_ssd_dt = jnp.pad(ssd_w_in[:, :, SSD_IN_MAIN:].astype(BF16), ((0, 0), (0, 0), (0, LANES - HEADS)))
    w_ssd_out = ssd_w_out.astype(BF16)
    w_cm_in = cm_w_in.astype(BF16)
    w_cm_out = cm_w_out.astype(BF16)
    w_up = mlp_w_up.astype(BF16)
    w_down = mlp_w_down.astype(BF16)

    ssm_p, conv_p, conv_s, v_s = [], [], [], []
    ssm_s = None
    for i in range(DEPTH):
        j = i // 2
        g_mix = row(norm_mix_g[i])
        if i % 2 == 0:
            shared = (ssd_conv_w[j], row(ssd_conv_b[j]), _pad_lanes(ssd_dt_bias[j]),
                      _pad_lanes(ssd_a_log[j]))
            d_wide = jnp.repeat(ssd_d[j], HEAD_DIM).reshape(1, D_INNER)
            n_g = row(ssd_norm_g[j])
            xp, s_p, c_p = _ssd_prompt(xp, batch, j, g_mix, w_ssd_in, w_ssd_dt, *shared, d_wide, n_g,
                                       w_ssd_out)
            ssm_p.append(s_p.reshape(batch, HEADS, HEAD_DIM, STATE))
            conv_p.append(c_p)
            xs, ssm_s, c_s = _ssd_sample(xs, state_ssm, state_conv, ssm_s, j, g_mix, w_ssd_in, w_ssd_dt,
                                         *shared, expand, d_wide, n_g, w_ssd_out)
            conv_s.append(c_s)
        else:
            b_in = row(cm_b_in[j])
            v_g = row(cm_v_norm_g[j])
            b_s_wide = jnp.repeat(cm_b_s[j].T, CM_GROUP_DIM, axis=1)
            w_diag = jnp.repeat(cm_w_s[j][:, pos, pos], CM_GROUP_DIM).reshape(1, CM_WIDTH)
            b_diag = jnp.repeat(cm_b_s[j][:, pos], CM_GROUP_DIM).reshape(1, CM_WIDTH)
            xp, xs, v = _cm(xp, xs, j, g_mix, w_cm_in, b_in, v_g, cm_w_s, b_s_wide, w_diag, b_diag,
                            w_cm_out)
            v_s.append(v.reshape(dec_batch, dec_seq, CM_WIDTH))
        xp, xs = _mlp(xp, xs, i, row(norm_mlp_g[i]), w_up, w_down, row(norm_final_g), i == DEPTH - 1)

    return (xp.reshape(batch, seq, D_MODEL), xs.reshape(dec_batch, dec_seq, D_MODEL),
            jnp.stack(ssm_p), jnp.stack(conv_p),
            ssm_s.reshape(state_ssm.shape), jnp.stack(conv_s), jnp.stack(v_s))
```
